```python
import math
import jax
import jax.numpy as jnp
from jax import lax
import numpy as np

D_MODEL = 1024
BATCH = 32
SEQ = 2048
DEPTH = 2

GRID_W = 64
CTX_LEN = 256
N_EVEN = (DEPTH + 1) // 2
N_ODD = DEPTH // 2
ADA_CHUNKS = 6
NORM_EPS = 1e-6
ROPE_THETA = 10000.0
ROPE_DIM = 64
ROPE_FREQS = ROPE_DIM // 4
Q_BLOCK = 128

A_HEADS = 4
A_DH = 64
A_DV = 2 * A_DH
LAMBDA_INIT_BASE = 0.8
LAMBDA_INIT_SPAN = 0.6
LAMBDA_INIT_RATE = 0.3
B_HEADS = 8
B_DK = 64
B_DV = 64
B_CHUNK = 32
C_HEADS = 4
C_DK = 128
C_DV = 128
C_CHUNK = 64
C_CONV = 5
D_HEADS = 8
D_KV_HEADS = 2
D_DH = 64
D_WINDOW = 128
P_HEADS = 8
P_NKEYS = 128
P_NEXPERTS = P_NKEYS * P_NKEYS
P_DQ = 256
P_TOPK = 16
P_TOKEN_BLOCK = 128

EVEN_SPLITS = (A_HEADS * 2 * A_DH, A_HEADS * 2 * A_DH, A_HEADS * A_DV,
               B_HEADS * B_DK, B_HEADS * B_DV, B_HEADS * B_DK, B_HEADS * B_DK, B_HEADS * B_DV)
EVEN_IN = sum(EVEN_SPLITS)
EVEN_OUT = A_HEADS * A_DV + B_HEADS * B_DV
C_QKV = C_HEADS * (2 * C_DK + C_DV)
ODD_SPLITS = (C_QKV, C_HEADS * C_DV, 4 * C_HEADS, D_HEADS * D_DH, 2 * D_KV_HEADS * D_DH)
ODD_IN = sum(ODD_SPLITS)
ODD_OUT = C_HEADS * C_DV + D_HEADS * D_DH

kernel_name = 'hybrid_diffusion_trunk_diffattn_hgrn2_gdn_swa_peer'


def split_cols(p, sizes):
    idx = [int(s) for s in np.cumsum(sizes)[:-1]]
    return jnp.split(p, idx, axis=-1)


def rmsnorm(x, w):
    xf = x.astype(jnp.float32)
    xf = xf * lax.rsqrt(jnp.mean(xf * xf, axis=-1, keepdims=True) + NORM_EPS)
    return (xf * w.astype(jnp.float32)).astype(x.dtype)


def l2norm(x):
    xf = x.astype(jnp.float32)
    return xf * lax.rsqrt(jnp.sum(xf * xf, axis=-1, keepdims=True) + NORM_EPS)


def modulate(x, w, shift, scale):
    return rmsnorm(x, w) * (1.0 + scale) + shift


def ada_split(cvec, w, b):
    return jnp.split(jax.nn.silu(cvec) @ w + b, ADA_CHUNKS, axis=-1)


def axial_rope_tables(rows):
    row = jnp.repeat(jnp.arange(rows, dtype=jnp.float32), GRID_W)
    col = jnp.tile(jnp.arange(GRID_W, dtype=jnp.float32), rows)
    inv = ROPE_THETA ** (-jnp.arange(ROPE_FREQS, dtype=jnp.float32) / ROPE_FREQS)
    ar = row[:, None] * inv[None, :]
    ac = col[:, None] * inv[None, :]
    ang = jnp.concatenate([ar, ar, ac, ac], axis=-1)
    return jnp.cos(ang), jnp.sin(ang)


def apply_rope(x, cos, sin):
    x1, x2, x3, x4 = jnp.split(x, 4, axis=-1)
    rot = jnp.concatenate([-x2, x1, -x4, x3], axis=-1)
    return (x * cos[:, None, :] + rot * sin[:, None, :]).astype(x.dtype)


def centred_depthwise_conv(x, w):
    pad = C_CONV // 2
    return lax.conv_general_dilated(x, w[:, None, :].astype(x.dtype), window_strides=(1,),
                                    padding=[(pad, pad)], dimension_numbers=('NWC', 'WIO', 'NWC'),
                                    feature_group_count=x.shape[-1])


def to_chunks(a, chunk):
    B, n = a.shape[:2]
    a = a.reshape((B, n // chunk, chunk) + a.shape[2:])
    return jnp.moveaxis(jnp.moveaxis(a, 1, 0), 2, 3)


def from_chunks(o, n):
    o = jnp.moveaxis(jnp.moveaxis(o, 3, 2), 0, 1)
    return o.reshape((o.shape[0], n) + o.shape[3:])


def gla_chunked(q, k, v, log_f, s0):
    n = q.shape[1]
    lower = jnp.tril(jnp.ones((B_CHUNK, B_CHUNK), dtype=bool))
    mid = B_CHUNK // 2

    def step(S, inp):
        qi, ki, vi, gi = inp
        b = jnp.cumsum(gi, axis=-2)
        ref = b[..., mid:mid + 1, :]
        att = jnp.einsum('bhqd,bhkd->bhqk', qi * jnp.exp(b - ref), ki * jnp.exp(ref - b))
        att = jnp.where(lower, att, 0.0)
        o = (jnp.einsum('bhqk,bhke->bhqe', att, vi)
             + jnp.einsum('bhqd,bhde->bhqe', qi * jnp.exp(b), S))
        b_last = b[..., -1:, :]
        S = (S * jnp.exp(b_last)[..., 0, :, None]
             + jnp.einsum('bhkd,bhke->bhde', ki * jnp.exp(b_last - b), vi))
        return S, o

    S, o = lax.scan(step, s0, tuple(to_chunks(a, B_CHUNK) for a in (q, k, v, log_f)))
    return from_chunks(o, n), S


def gated_delta_chunked(q, k, v, log_alpha, beta, s0):
    n = q.shape[1]
    dv = v.shape[-1]
    incl = jnp.tril(jnp.ones((C_CHUNK, C_CHUNK), dtype=bool))
    strict = jnp.tril(jnp.ones((C_CHUNK, C_CHUNK), dtype=bool), k=-1)
    eye = jnp.eye(C_CHUNK, dtype=jnp.float32)

    def step(S, inp):
        qi, ki, vi, gi, bi = inp
        g = jnp.cumsum(gi, axis=-1)
        decay = jnp.exp(jnp.where(incl, g[..., :, None] - g[..., None, :], -jnp.inf))
        kb = ki * bi[..., None]
        a_mat = jnp.where(strict, jnp.einsum('bhid,bhjd->bhij', kb, ki) * decay, 0.0) + eye
        rhs = jnp.concatenate([vi * bi[..., None], kb * jnp.exp(g)[..., None]], axis=-1)
        sol = lax.linalg.triangular_solve(a_mat, rhs, left_side=True, lower=True, unit_diagonal=True)
        u, w = sol[..., :dv], sol[..., dv:]
        v_new = u - jnp.einsum('bhcd,bhde->bhce', w, S)
        att = jnp.einsum('bhid,bhjd->bhij', qi, ki) * decay
        o = (jnp.einsum('bhcd,bhde->bhce', qi * jnp.exp(g)[..., None], S)
             + jnp.einsum('bhij,bhje->bhie', att, v_new))
        g_last = g[..., -1:]
        S = (S * jnp.exp(g_last)[..., None]
             + jnp.einsum('bhcd,bhce->bhde', ki * jnp.exp(g_last - g)[..., None], v_new))
        return S, o

    S, o = lax.scan(step, s0, tuple(to_chunks(a, C_CHUNK) for a in (q, k, v, log_alpha, beta)))
    return from_chunks(o, n), S


def bidirectional_scan(scan_fn, lat_fwd, lat_bwd, ctx_fwd, ctx_bwd, s0):
    o_lat, o_ctx = [], []
    for lat, ctxs, rev in ((lat_fwd, ctx_fwd, False), (lat_bwd, ctx_bwd, True)):
        if rev:
            lat = tuple(jnp.flip(a, axis=1) for a in lat)
            ctxs = tuple(jnp.flip(a, axis=1) for a in ctxs)
        oc, s_ctx = scan_fn(*ctxs, s0)
        ol, _ = scan_fn(*lat, s_ctx)
        if rev:
            oc, ol = jnp.flip(oc, axis=1), jnp.flip(ol, axis=1)
        o_lat.append(ol)
        o_ctx.append(oc)
    return o_lat[0] + o_lat[1], o_ctx[0] + o_ctx[1]


def diff_attn_probs(q, k, lam):
    s = jnp.einsum('bqhmd,bkhmd->bhmqk', q, k).astype(jnp.float32) * (A_DH ** -0.5)
    p = jax.nn.softmax(s, axis=-1)
    return p[:, :, 0] - lam * p[:, :, 1]


def diff_attention_latent(q, k, v, kc, vc, lam):
    B, T, H = q.shape[:3]
    k_all = jnp.concatenate([kc, k], axis=1)
    v_all = jnp.concatenate([vc, v], axis=1)
    nb = T // Q_BLOCK
    qb = jnp.moveaxis(q.reshape(B, nb, Q_BLOCK, H, 2, A_DH), 1, 0)

    def block(qi):
        a = diff_attn_probs(qi, k_all, lam)
        return jnp.einsum('bhqk,bkhe->bqhe', a.astype(v_all.dtype), v_all)

    o = lax.map(block, qb)
    return jnp.moveaxis(o, 0, 1).reshape(B, T, H, A_DV)


def diff_attention_ctx(qc, kc, vc, lam):
    a = diff_attn_probs(qc, kc, lam)
    return jnp.einsum('bhqk,bkhe->bqhe', a.astype(vc.dtype), vc)


def window_attention_latent(q, k, v, kc, vc, sink):
    B, T, H, d = q.shape
    G = H // D_KV_HEADS
    L = kc.shape[1]
    nb = T // Q_BLOCK
    band = Q_BLOCK + 2 * D_WINDOW
    pad = ((0, 0), (D_WINDOW, D_WINDOW), (0, 0), (0, 0))
    kp, vp = jnp.pad(k, pad), jnp.pad(v, pad)
    qb = jnp.moveaxis(q.reshape(B, nb, Q_BLOCK, D_KV_HEADS, G, d), 1, 0)
    rel = jnp.arange(band)[None, :] - D_WINDOW - jnp.arange(Q_BLOCK)[:, None]
    in_window = jnp.abs(rel) <= D_WINDOW
    sink_l = sink.astype(jnp.float32).reshape(1, D_KV_HEADS, G, 1, 1)
    scale = d ** -0.5

    def block(args):
        i, qi = args
        start = i * Q_BLOCK
        kb = lax.dynamic_slice_in_dim(kp, start, band, axis=1)
        vb = lax.dynamic_slice_in_dim(vp, start, band, axis=1)
        key_pos = start - D_WINDOW + jnp.arange(band)
        mask = in_window & ((key_pos >= 0) & (key_pos < T))[None, :]
        s_band = jnp.einsum('bqkgd,bnkd->bkgqn', qi, kb).astype(jnp.float32) * scale
        s_band = jnp.where(mask, s_band, -jnp.inf)
        s_ctx = jnp.einsum('bqkgd,bnkd->bkgqn', qi, kc).astype(jnp.float32) * scale
        logits = jnp.concatenate([jnp.broadcast_to(sink_l, s_ctx.shape[:-1] + (1,)), s_ctx, s_band], axis=-1)
        p = jax.nn.softmax(logits, axis=-1).astype(v.dtype)
        return (jnp.einsum('bkgqn,bnkd->bqkgd', p[..., 1:1 + L], vc)
                + jnp.einsum('bkgqn,bnkd->bqkgd', p[..., 1 + L:], vb))

    o = lax.map(block, (jnp.arange(nb), qb))
    return jnp.moveaxis(o, 0, 1).reshape(B, T, H, d)


def window_attention_ctx(qc, kc, vc, sink):
    B, L, H, d = qc.shape
    G = H // D_KV_HEADS
    s = jnp.einsum('bqkgd,bnkd->bkgqn', qc.reshape(B, L, D_KV_HEADS, G, d), kc).astype(jnp.float32) * d ** -0.5
    sink_l = jnp.broadcast_to(sink.astype(jnp.float32).reshape(1, D_KV_HEADS, G, 1, 1), s.shape[:-1] + (1,))
    p = jax.nn.softmax(jnp.concatenate([sink_l, s], axis=-1), axis=-1)[..., 1:]
    return jnp.einsum('bkgqn,bnkd->bqkgd', p.astype(vc.dtype), vc).reshape(B, L, H, d)


def peer_ffn(h, w_q, keys, u_tab, v_tab):
    B, T, D = h.shape
    hb = h.reshape(B * T // P_TOKEN_BLOCK, P_TOKEN_BLOCK, D)

    def block(xb):
        q = (xb @ w_q).reshape(P_TOKEN_BLOCK, P_HEADS, 2, P_DQ // 2)
        s = jnp.einsum('nhpd,hpkd->nhpk', q, keys).astype(jnp.float32)
        s_top, i_top = lax.top_k(s, P_TOPK)
        cand = (s_top[:, :, 0, :, None] + s_top[:, :, 1, None, :]).reshape(P_TOKEN_BLOCK, P_HEADS, P_TOPK * P_TOPK)
        cand_idx = (i_top[:, :, 0, :, None] * P_NKEYS + i_top[:, :, 1, None, :]).reshape(P_TOKEN_BLOCK, P_HEADS, P_TOPK * P_TOPK)
        best, pos = lax.top_k(cand, P_TOPK)
        idx = jnp.take_along_axis(cand_idx, pos, axis=-1)
        gate = jax.nn.softmax(best, axis=-1)
        u = u_tab[idx]
        v = v_tab[idx]
        act = jax.nn.gelu(jnp.einsum('nd,nhkd->nhk', xb, u).astype(jnp.float32), approximate=False)
        return jnp.einsum('nhk,nhkd->nd', (gate * act).astype(v.dtype), v)

    return lax.map(block, hb).reshape(B, T, D)


def even_mixer(layer, h, hc, w_in, w_out, diff_lambda, subln_w, hgrn_lb, hgrn_norm_w, cos, sin, need_ctx):
    B, T, _ = h.shape
    L = hc.shape[1]
    q_a, k_a, v_a, q_b, i_b, ff_b, fb_b, g_b = split_cols(h @ w_in, EVEN_SPLITS)
    qc_a, kc_a, vc_a, qc_b, ic_b, ffc_b, fbc_b, gc_b = split_cols(hc @ w_in, EVEN_SPLITS)

    lambda_init = LAMBDA_INIT_BASE - LAMBDA_INIT_SPAN * math.exp(-LAMBDA_INIT_RATE * layer)
    lam_f = diff_lambda.astype(jnp.float32)
    lam = jnp.exp(jnp.sum(lam_f[0] * lam_f[1])) - jnp.exp(jnp.sum(lam_f[2] * lam_f[3])) + lambda_init

    def qk_latent(t):
        return apply_rope(t.reshape(B, T, 2 * A_HEADS, A_DH), cos, sin).reshape(B, T, A_HEADS, 2, A_DH)

    kc = kc_a.reshape(B, L, A_HEADS, 2, A_DH)
    vc = vc_a.reshape(B, L, A_HEADS, A_DV)
    o_a = diff_attention_latent(qk_latent(q_a), qk_latent(k_a), v_a.reshape(B, T, A_HEADS, A_DV), kc, vc, lam)
    o_a = (rmsnorm(o_a, subln_w) * (1.0 - lambda_init)).astype(h.dtype)

    lb = jnp.cumsum(jax.nn.softmax(hgrn_lb.astype(jnp.float32), axis=0), axis=0)[layer]

    def hgrn_dirs(q, i, ff, fb, n):
        q = (jax.nn.silu(q.astype(jnp.float32)) * B_DK ** -0.5).reshape(B, n, B_HEADS, B_DK)
        i = i.astype(jnp.float32).reshape(B, n, B_HEADS, B_DV)
        out = []
        for f_logit in (ff, fb):
            f = (lb + (1.0 - lb) * jax.nn.sigmoid(f_logit.astype(jnp.float32))).reshape(B, n, B_HEADS, B_DK)
            out.append((q, 1.0 - f, i, jnp.log(f)))
        return out

    lat_f, lat_b = hgrn_dirs(q_b, i_b, ff_b, fb_b, T)
    ctx_f, ctx_b = hgrn_dirs(qc_b, ic_b, ffc_b, fbc_b, L)
    s0 = jnp.zeros((B, B_HEADS, B_DK, B_DV), jnp.float32)
    o_b, oc_b = bidirectional_scan(gla_chunked, lat_f, lat_b, ctx_f, ctx_b, s0)

    def hgrn_out(o, g, n):
        gate = jax.nn.silu(g.astype(jnp.float32).reshape(B, n, B_HEADS, B_DV))
        return (rmsnorm(o, hgrn_norm_w) * gate).reshape(B, n, B_HEADS * B_DV)

    y = jnp.concatenate([o_a.reshape(B, T, A_HEADS * A_DV), hgrn_out(o_b, g_b, T).astype(h.dtype)], axis=-1) @ w_out
    if not need_ctx:
        return y, None
    oc_a = rmsnorm(diff_attention_ctx(qc_a.reshape(B, L, A_HEADS, 2, A_DH), kc, vc, lam), subln_w) * (1.0 - lambda_init)
    yc = jnp.concatenate([oc_a.reshape(B, L, A_HEADS * A_DV).astype(hc.dtype),
                          hgrn_out(oc_b, gc_b, L).astype(hc.dtype)], axis=-1) @ w_out
    return y, yc


def odd_mixer(h, hc, w_in, w_out, conv_w, a_log, dt_bias, gdn_norm_w, sink, cos, sin, need_ctx):
    B, T, _ = h.shape
    L = hc.shape[1]
    qkv, z, gates, q_d, kv_d = split_cols(h @ w_in, ODD_SPLITS)
    qkvc, zc, gatesc, qc_d, kvc_d = split_cols(hc @ w_in, ODD_SPLITS)

    a_log_f = a_log.astype(jnp.float32)
    dt_f = dt_bias.astype(jnp.float32)

    def gdn_dirs(qkv_t, gates_t, n):
        qkv_t = jax.nn.silu(centred_depthwise_conv(qkv_t, conv_w))
        q, k, v = split_cols(qkv_t, (C_HEADS * C_DK, C_HEADS * C_DK, C_HEADS * C_DV))
        q = l2norm(q.reshape(B, n, C_HEADS, C_DK)) * C_DK ** -0.5
        k = l2norm(k.reshape(B, n, C_HEADS, C_DK))
        v = v.astype(jnp.float32).reshape(B, n, C_HEADS, C_DV)
        a_f, a_b, b_f, b_b = jnp.split(gates_t.astype(jnp.float32), 4, axis=-1)
        out = []
        for d, (a_t, b_t) in enumerate(((a_f, b_f), (a_b, b_b))):
            log_alpha = -jnp.exp(a_log_f[d]) * jax.nn.softplus(a_t + dt_f[d])
            out.append((q, k, v, log_alpha, jax.nn.sigmoid(b_t)))
        return out

    lat_f, lat_b = gdn_dirs(qkv, gates, T)
    ctx_f, ctx_b = gdn_dirs(qkvc, gatesc, L)
    s0 = jnp.zeros((B, C_HEADS, C_DK, C_DV), jnp.float32)
    o_c, oc_c = bidirectional_scan(gated_delta_chunked, lat_f, lat_b, ctx_f, ctx_b, s0)

    def gdn_out(o, zt, n):
        gate = jax.nn.silu(zt.astype(jnp.float32).reshape(B, n, C_HEADS, C_DV))
        return (rmsnorm(o, gdn_norm_w) * gate).reshape(B, n, C_HEADS * C_DV)

    k_d, v_d = jnp.split(kv_d, 2, axis=-1)
    kc_d, vc_d = jnp.split(kvc_d, 2, axis=-1)
    q_lat = apply_rope(q_d.reshape(B, T, D_HEADS, D_DH), cos, sin)
    k_lat = apply_rope(k_d.reshape(B, T, D_KV_HEADS, D_DH), cos, sin)
    kc = kc_d.reshape(B, L, D_KV_HEADS, D_DH)
    vc = vc_d.reshape(B, L, D_KV_HEADS, D_DH)
    o_d = window_attention_latent(q_lat, k_lat, v_d.reshape(B, T, D_KV_HEADS, D_DH), kc, vc, sink)

    y = jnp.concatenate([gdn_out(o_c, z, T).astype(h.dtype), o_d.reshape(B, T, D_HEADS * D_DH)], axis=-1) @ w_out
    if not need_ctx:
        return y, None
    oc_d = window_attention_ctx(qc_d.reshape(B, L, D_HEADS, D_DH), kc, vc, sink)
    yc = jnp.concatenate([gdn_out(oc_c, zc, L).astype(hc.dtype), oc_d.reshape(B, L, D_HEADS * D_DH)], axis=-1) @ w_out
    return y, yc


def setup_inputs(seed: int = 0) -> dict:
    key = jax.random.key(seed)
    ks = jax.random.split(key, 26)
    f32 = jnp.float32

    def nrm(k, shape, scale):
        return jax.random.normal(k, shape, f32) * scale

    dt = jnp.exp(jax.random.uniform(ks[18], (N_ODD, 2, C_HEADS), f32, math.log(1e-3), math.log(1e-1)))
    return {
        'x': nrm(ks[0], (BATCH, SEQ, D_MODEL), 1.0),
        'c': nrm(ks[1], (BATCH, D_MODEL), 1.0),
        'ctx': nrm(ks[2], (BATCH, CTX_LEN, D_MODEL), 1.0),
        'c_ctx': nrm(ks[3], (D_MODEL,), 1.0),
        'ada_w': nrm(ks[4], (DEPTH, D_MODEL, ADA_CHUNKS * D_MODEL), 0.5 * D_MODEL ** -0.5),
        'ada_b': nrm(ks[5], (DEPTH, ADA_CHUNKS * D_MODEL), 0.02),
        'norm_w': 1.0 + nrm(ks[6], (DEPTH, 2, D_MODEL), 0.02),
        'final_norm_w': 1.0 + nrm(ks[7], (D_MODEL,), 0.02),
        'even_w_in': nrm(ks[8], (N_EVEN, D_MODEL, EVEN_IN), D_MODEL ** -0.5),
        'even_w_out': nrm(ks[9], (N_EVEN, EVEN_OUT, D_MODEL), EVEN_OUT ** -0.5),
        'diff_lambda': nrm(ks[10], (N_EVEN, 4, A_DH), 0.1),
        'diff_subln_w': 1.0 + nrm(ks[11], (N_EVEN, A_DV), 0.02),
        'hgrn_lb': nrm(ks[12], (DEPTH + 1, B_HEADS * B_DK), 0.1),
        'hgrn_norm_w': 1.0 + nrm(ks[13], (N_EVEN, B_DV), 0.02),
        'odd_w_in': nrm(ks[14], (N_ODD, D_MODEL, ODD_IN), D_MODEL ** -0.5),
        'odd_w_out': nrm(ks[15], (N_ODD, ODD_OUT, D_MODEL), ODD_OUT ** -0.5),
        'gdn_conv_w': nrm(ks[16], (N_ODD, C_CONV, C_QKV), C_CONV ** -0.5),
        'gdn_a_log': jnp.log(jax.random.uniform(ks[17], (N_ODD, 2, C_HEADS), f32, 1.0, 16.0)),
        'gdn_dt_bias': dt + jnp.log(-jnp.expm1(-dt)),
        'gdn_norm_w': 1.0 + nrm(ks[19], (N_ODD, C_DV), 0.02),
        'swa_sink': nrm(ks[20], (N_ODD, D_HEADS), 0.5),
        'peer_w_q': nrm(ks[21], (DEPTH, D_MODEL, P_HEADS * P_DQ), D_MODEL ** -0.5),
        'peer_keys': nrm(ks[22], (DEPTH, P_HEADS, 2, P_NKEYS, P_DQ // 2), (P_DQ // 2) ** -0.5),
        'peer_u': nrm(ks[23], (DEPTH, P_NEXPERTS, D_MODEL), D_MODEL ** -0.5),
        'peer_v': nrm(ks[24], (DEPTH, P_NEXPERTS, D_MODEL), 0.25),
    }


def reference(x, c, ctx, c_ctx, ada_w, ada_b, norm_w, final_norm_w, even_w_in, even_w_out, diff_lambda,
              diff_subln_w, hgrn_lb, hgrn_norm_w, odd_w_in, odd_w_out, gdn_conv_w, gdn_a_log, gdn_dt_bias,
              gdn_norm_w, swa_sink, peer_w_q, peer_keys, peer_u, peer_v):
    T = x.shape[1]
    ROWS = T // GRID_W
    cos, sin = axial_rope_tables(ROWS)
    xc = ctx
    for layer in range(DEPTH):
        need_ctx = layer < DEPTH - 1
        w_ada, b_ada = ada_w[layer], ada_b[layer]
        sh1, sc1, g1, sh2, sc2, g2 = [m[:, None, :] for m in ada_split(c, w_ada, b_ada)]
        sh1c, sc1c, g1c, sh2c, sc2c, g2c = ada_split(c_ctx, w_ada, b_ada)
        h = modulate(x, norm_w[layer, 0], sh1, sc1)
        hc = modulate(xc, norm_w[layer, 0], sh1c, sc1c)
        if layer % 2 == 0:
            e = layer // 2
            y, yc = even_mixer(layer, h, hc, even_w_in[e], even_w_out[e], diff_lambda[e], diff_subln_w[e],
                               hgrn_lb, hgrn_norm_w[e], cos, sin, need_ctx)
        else:
            o = layer // 2
            y, yc = odd_mixer(h, hc, odd_w_in[o], odd_w_out[o], gdn_conv_w[o], gdn_a_log[o], gdn_dt_bias[o],
                              gdn_norm_w[o], swa_sink[o], cos, sin, need_ctx)
        x = x + g1 * y
        x = x + g2 * peer_ffn(modulate(x, norm_w[layer, 1], sh2, sc2),
                              peer_w_q[layer], peer_keys[layer], peer_u[layer], peer_v[layer])
        if need_ctx:
            xc = xc + g1c * yc
            xc = xc + g2c * peer_ffn(modulate(xc, norm_w[layer, 1], sh2c, sc2c),
                                     peer_w_q[layer], peer_keys[layer], peer_u[layer], peer_v[layer])
    return rmsnorm(x, final_norm_w)
```

```python
import functools
import math

import jax
import jax.numpy as jnp
import numpy as np
from jax import lax
from jax.experimental import pallas as pl
from jax.experimental.pallas import tpu as pltpu

D_MODEL = 1024
DEPTH = 2
GRID_W = 64
ADA_CHUNKS = 6
NORM_EPS = 1e-6
ROPE_THETA = 10000.0
ROPE_DIM = 64
ROPE_FREQS = ROPE_DIM // 4
Q_BLOCK = 128

A_HEADS = 4
A_DH = 64
A_DV = 2 * A_DH
LAMBDA_INIT_BASE = 0.8
LAMBDA_INIT_SPAN = 0.6
LAMBDA_INIT_RATE = 0.3
B_HEADS = 8
B_DK = 64
B_DV = 64
B_CHUNK = 32
C_HEADS = 4
C_DK = 128
C_DV = 128
C_CHUNK = 64
C_CONV = 5
D_HEADS = 8
D_KV_HEADS = 2
D_DH = 64
D_WINDOW = 128
P_HEADS = 8
P_NKEYS = 128
P_DQ = 256
P_TOPK = 16

EVEN_SPLITS = (A_HEADS * 2 * A_DH, A_HEADS * 2 * A_DH, A_HEADS * A_DV,
               B_HEADS * B_DK, B_HEADS * B_DV, B_HEADS * B_DK, B_HEADS * B_DK, B_HEADS * B_DV)
C_QKV = C_HEADS * (2 * C_DK + C_DV)
ODD_SPLITS = (C_QKV, C_HEADS * C_DV, 4 * C_HEADS, D_HEADS * D_DH, 2 * D_KV_HEADS * D_DH)

SUBLANES = 8
LANES = 128

P_NSEL = P_HEADS * P_TOPK
ROUTE_TN = 256
APPLY_TB = 8
ROW_TILES = 2 * D_MODEL // LANES
ROW_GROUPS = P_NSEL // SUBLANES
APPLY_VMEM_BYTES = 48 * 1024 * 1024


def _top16_rows(s, payload=None):
    rows = s.shape[0]
    iota = lax.broadcasted_iota(jnp.int32, s.shape, 0)
    vals, poss, pays = [], [], []
    for _ in range(P_TOPK):
        m = jnp.max(s, axis=0, keepdims=True)
        pos = jnp.min(jnp.where(s == m, iota, rows), axis=0, keepdims=True)
        hit = iota == pos
        if payload is not None:
            pays.append(jnp.max(jnp.where(hit, payload, -1), axis=0, keepdims=True))
        vals.append(m)
        poss.append(pos)
        s = jnp.where(hit, -jnp.inf, s)
    cat = lambda xs: jnp.concatenate(xs, axis=0)
    return cat(vals), cat(poss), (cat(pays) if payload is not None else None)


def _route_kernel(x_ref, sh_ref, sc_ref, nw_ref, wqt_ref, keys_ref, h_ref, idx_ref, gate_ref):
    x = x_ref[...]
    h = x * lax.rsqrt(jnp.mean(x * x, axis=-1, keepdims=True) + NORM_EPS) * nw_ref[...]
    h = h * (1.0 + sc_ref[0]) + sh_ref[0]
    h_ref[...] = h
    qt = lax.dot_general(wqt_ref[...], h, (((1,), (1,)), ((), ())),
                         precision=lax.Precision.HIGHEST, preferred_element_type=jnp.float32)
    idx_rows, gate_rows = [], []
    for hh in range(P_HEADS):
        tops = []
        for p in range(2):
            j = hh * 2 + p
            s = lax.dot_general(keys_ref[j], qt[j * P_NKEYS:(j + 1) * P_NKEYS, :], (((1,), (0,)), ((), ())),
                                precision=lax.Precision.HIGHEST, preferred_element_type=jnp.float32)
            v, i, _ = _top16_rows(s)
            tops.append((v, i))
        (v0, i0), (v1, i1) = tops
        cand = jnp.concatenate([v0[a:a + 1, :] + v1 for a in range(P_TOPK)], axis=0)
        cidx = jnp.concatenate([i0[a:a + 1, :] * P_NKEYS + i1 for a in range(P_TOPK)], axis=0)
        best, _, sel = _top16_rows(cand, cidx)
        e = jnp.exp(best - best[0:1, :])
        gate_rows.append(e / jnp.sum(e, axis=0, keepdims=True))
        idx_rows.append(sel)
    idx_t = jnp.concatenate(idx_rows, axis=0)
    gate_t = jnp.concatenate(gate_rows, axis=0)
    idx_ref[...] = idx_t.T
    for j in range(ROUTE_TN // APPLY_TB):
        gate_ref[j] = gate_t[:, j * APPLY_TB:(j + 1) * APPLY_TB]


def _peer_route(x2, shift, scale, norm_w, wq_t, keys, tokens_per_batch):
    n_tok = x2.shape[0]
    assert n_tok % ROUTE_TN == 0 and tokens_per_batch % ROUTE_TN == 0
    tiles_per_batch = tokens_per_batch // ROUTE_TN
    nkeys2 = 2 * P_HEADS
    return pl.pallas_call(
        _route_kernel,
        grid=(n_tok // ROUTE_TN,),
        in_specs=[
            pl.BlockSpec((ROUTE_TN, D_MODEL), lambda i: (i, 0)),
            pl.BlockSpec((1, 1, D_MODEL), lambda i: (i // tiles_per_batch, 0, 0)),
            pl.BlockSpec((1, 1, D_MODEL), lambda i: (i // tiles_per_batch, 0, 0)),
            pl.BlockSpec((1, D_MODEL), lambda i: (0, 0)),
            pl.BlockSpec((P_HEADS * P_DQ, D_MODEL), lambda i: (0, 0)),
            pl.BlockSpec((nkeys2, P_NKEYS, P_DQ // 2), lambda i: (0, 0, 0)),
        ],
        out_specs=[
            pl.BlockSpec((ROUTE_TN, D_MODEL), lambda i: (i, 0)),
            pl.BlockSpec((ROUTE_TN, P_NSEL), lambda i: (i, 0)),
            pl.BlockSpec((ROUTE_TN // APPLY_TB, P_NSEL, APPLY_TB), lambda i: (i, 0, 0)),
        ],
        out_shape=[
            jax.ShapeDtypeStruct((n_tok, D_MODEL), jnp.float32),
            jax.ShapeDtypeStruct((n_tok, P_NSEL), jnp.int32),
            jax.ShapeDtypeStruct((n_tok // APPLY_TB, P_NSEL, APPLY_TB), jnp.float32),
        ],
        compiler_params=pltpu.CompilerParams(dimension_semantics=("arbitrary",),
                                             vmem_limit_bytes=48 * 1024 * 1024),
        name="peer_route",
    )(x2, shift, scale, norm_w.reshape(1, D_MODEL), wq_t, keys.reshape(nkeys2, P_NKEYS, P_DQ // 2))


def _apply_kernel(idx_cur, idx_nxt, h_ref, gate_ref, tab_ref, o_ref, buf, sem):
    i = pl.program_id(0)
    n = pl.num_programs(0)
    slot = i % 2

    def row_copy(idx_ref, s, t, r):
        return pltpu.make_async_copy(
            tab_ref.at[idx_ref[t, r]],
            buf.at[s, t * ROW_GROUPS + r // SUBLANES, :, r % SUBLANES, :],
            sem.at[s])

    def issue(idx_ref, s):
        def body(t, carry):
            for r in range(P_NSEL):
                row_copy(idx_ref, s, t, r).start()
            return carry
        lax.fori_loop(0, APPLY_TB, body, 0)

    @pl.when(i == 0)
    def _():
        issue(idx_cur, 0)

    @pl.when(i + 1 < n)
    def _():
        issue(idx_nxt, 1 - slot)

    pltpu.make_async_copy(buf.at[slot], buf.at[slot], sem.at[slot]).wait()

    half = ROW_TILES // 2
    lane = lax.broadcasted_iota(jnp.int32, (P_NSEL, APPLY_TB), 1)
    a = jnp.zeros((P_NSEL, APPLY_TB), jnp.float32)
    for t in range(APPLY_TB):
        u = buf[slot, t * ROW_GROUPS:(t + 1) * ROW_GROUPS, :half]
        xt = h_ref[t:t + 1, :].reshape(1, half, 1, LANES)
        s = jnp.sum(u * xt, axis=1)
        at = jnp.sum(s, axis=-1, keepdims=True).reshape(P_NSEL, 1)
        a = jnp.where(lane == t, at, a)
    coef = gate_ref[0] * (0.5 * a * (1.0 + lax.erf(a * (2.0 ** -0.5))))
    for t in range(APPLY_TB):
        v = buf[slot, t * ROW_GROUPS:(t + 1) * ROW_GROUPS, half:]
        ct = coef[:, t:t + 1].reshape(ROW_GROUPS, 1, SUBLANES, 1)
        o = jnp.sum(jnp.sum(ct * v, axis=0), axis=1)
        o_ref[t:t + 1, :] = o.reshape(1, D_MODEL)


def _peer_apply(idx, h, gate_t, tab):
    n_tok = h.shape[0]
    nt = n_tok // APPLY_TB
    return pl.pallas_call(
        _apply_kernel,
        grid=(nt,),
        in_specs=[
            pl.BlockSpec((APPLY_TB, P_NSEL), lambda i: (i, 0), memory_space=pltpu.SMEM),
            pl.BlockSpec((APPLY_TB, P_NSEL), lambda i: (jnp.minimum(i + 1, nt - 1), 0), memory_space=pltpu.SMEM),
            pl.BlockSpec((APPLY_TB, D_MODEL), lambda i: (i, 0)),
            pl.BlockSpec((1, P_NSEL, APPLY_TB), lambda i: (i, 0, 0)),
            pl.BlockSpec(memory_space=pl.ANY),
        ],
        out_specs=pl.BlockSpec((APPLY_TB, D_MODEL), lambda i: (i, 0)),
        out_shape=jax.ShapeDtypeStruct((n_tok, D_MODEL), jnp.float32),
        scratch_shapes=[pltpu.VMEM((2, APPLY_TB * ROW_GROUPS, ROW_TILES, SUBLANES, LANES), jnp.float32),
                        pltpu.SemaphoreType.DMA((2,))],
        compiler_params=pltpu.CompilerParams(dimension_semantics=("arbitrary",),
                                             vmem_limit_bytes=APPLY_VMEM_BYTES),
        name="peer_apply",
    )(idx, idx, h, gate_t, tab)


def _peer_ffn(x, norm_w, shift, scale, wq_t, keys, tab):
    B, T, D = x.shape
    h, idx, gate_t = _peer_route(x.reshape(B * T, D), shift, scale, norm_w, wq_t, keys, T)
    return _peer_apply(idx, h, gate_t, tab).reshape(B, T, D)


def _split_cols(p, sizes):
    idx = [int(s) for s in np.cumsum(sizes)[:-1]]
    return jnp.split(p, idx, axis=-1)


def _rmsnorm(x, w):
    xf = x.astype(jnp.float32)
    xf = xf * lax.rsqrt(jnp.mean(xf * xf, axis=-1, keepdims=True) + NORM_EPS)
    return (xf * w.astype(jnp.float32)).astype(x.dtype)


def _l2norm(x):
    xf = x.astype(jnp.float32)
    return xf * lax.rsqrt(jnp.sum(xf * xf, axis=-1, keepdims=True) + NORM_EPS)


def _modulate(x, w, shift, scale):
    return _rmsnorm(x, w) * (1.0 + scale) + shift


def _ada_split(cvec, w, b):
    return jnp.split(jax.nn.silu(cvec) @ w + b, ADA_CHUNKS, axis=-1)


def _axial_rope_tables(rows):
    row = jnp.repeat(jnp.arange(rows, dtype=jnp.float32), GRID_W)
    col = jnp.tile(jnp.arange(GRID_W, dtype=jnp.float32), rows)
    inv = ROPE_THETA ** (-jnp.arange(ROPE_FREQS, dtype=jnp.float32) / ROPE_FREQS)
    ar = row[:, None] * inv[None, :]
    ac = col[:, None] * inv[None, :]
    ang = jnp.concatenate([ar, ar, ac, ac], axis=-1)
    return jnp.cos(ang), jnp.sin(ang)


def _apply_rope(x, cos, sin):
    x1, x2, x3, x4 = jnp.split(x, 4, axis=-1)
    rot = jnp.concatenate([-x2, x1, -x4, x3], axis=-1)
    return (x * cos[:, None, :] + rot * sin[:, None, :]).astype(x.dtype)


def _centred_depthwise_conv(x, w):
    pad = C_CONV // 2
    return lax.conv_general_dilated(x, w[:, None, :].astype(x.dtype), window_strides=(1,),
                                    padding=[(pad, pad)], dimension_numbers=('NWC', 'WIO', 'NWC'),
                                    feature_group_count=x.shape[-1])


def _to_chunks(a, chunk):
    B, n = a.shape[:2]
    a = a.reshape((B, n // chunk, chunk) + a.shape[2:])
    return jnp.moveaxis(jnp.moveaxis(a, 1, 0), 2, 3)


def _from_chunks(o, n):
    o = jnp.moveaxis(jnp.moveaxis(o, 3, 2), 0, 1)
    return o.reshape((o.shape[0], n) + o.shape[3:])


def _gla_chunked(q, k, v, log_f, s0):
    n = q.shape[1]
    lower = jnp.tril(jnp.ones((B_CHUNK, B_CHUNK), dtype=bool))
    mid = B_CHUNK // 2

    def step(S, inp):
        qi, ki, vi, gi = inp
        b = jnp.cumsum(gi, axis=-2)
        ref = b[..., mid:mid + 1, :]
        att = jnp.einsum('bhqd,bhkd->bhqk', qi * jnp.exp(b - ref), ki * jnp.exp(ref - b))
        att = jnp.where(lower, att, 0.0)
        o = (jnp.einsum('bhqk,bhke->bhqe', att, vi)
             + jnp.einsum('bhqd,bhde->bhqe', qi * jnp.exp(b), S))
        b_last = b[..., -1:, :]
        S = (S * jnp.exp(b_last)[..., 0, :, None]
             + jnp.einsum('bhkd,bhke->bhde', ki * jnp.exp(b_last - b), vi))
        return S, o

    S, o = lax.scan(step, s0, tuple(_to_chunks(a, B_CHUNK) for a in (q, k, v, log_f)))
    return _from_chunks(o, n), S


def _gated_delta_chunked(q, k, v, log_alpha, beta, s0):
    n = q.shape[1]
    dv = v.shape[-1]
    incl = jnp.tril(jnp.ones((C_CHUNK, C_CHUNK), dtype=bool))
    strict = jnp.tril(jnp.ones((C_CHUNK, C_CHUNK), dtype=bool), k=-1)
    eye = jnp.eye(C_CHUNK, dtype=jnp.float32)

    def step(S, inp):
        qi, ki, vi, gi, bi = inp
        g = jnp.cumsum(gi, axis=-1)
        decay = jnp.exp(jnp.where(incl, g[..., :, None] - g[..., None, :], -jnp.inf))
        kb = ki * bi[..., None]
        a_mat = jnp.where(strict, jnp.einsum('bhid,bhjd->bhij', kb, ki) * decay, 0.0) + eye
        rhs = jnp.concatenate([vi * bi[..., None], kb * jnp.exp(g)[..., None]], axis=-1)
        sol = lax.linalg.triangular_solve(a_mat, rhs, left_side=True, lower=True, unit_diagonal=True)
        u, w = sol[..., :dv], sol[..., dv:]
        v_new = u - jnp.einsum('bhcd,bhde->bhce', w, S)
        att = jnp.einsum('bhid,bhjd->bhij', qi, ki) * decay
        o = (jnp.einsum('bhcd,bhde->bhce', qi * jnp.exp(g)[..., None], S)
             + jnp.einsum('bhij,bhje->bhie', att, v_new))
        g_last = g[..., -1:]
        S = (S * jnp.exp(g_last)[..., None]
             + jnp.einsum('bhcd,bhce->bhde', ki * jnp.exp(g_last - g)[..., None], v_new))
        return S, o

    S, o = lax.scan(step, s0, tuple(_to_chunks(a, C_CHUNK) for a in (q, k, v, log_alpha, beta)))
    return _from_chunks(o, n), S


def _bidirectional_scan(scan_fn, lat_fwd, lat_bwd, ctx_fwd, ctx_bwd, s0):
    o_lat, o_ctx = [], []
    for lat, ctxs, rev in ((lat_fwd, ctx_fwd, False), (lat_bwd, ctx_bwd, True)):
        if rev:
            lat = tuple(jnp.flip(a, axis=1) for a in lat)
            ctxs = tuple(jnp.flip(a, axis=1) for a in ctxs)
        oc, s_ctx = scan_fn(*ctxs, s0)
        ol, _ = scan_fn(*lat, s_ctx)
        if rev:
            oc, ol = jnp.flip(oc, axis=1), jnp.flip(ol, axis=1)
        o_lat.append(ol)
        o_ctx.append(oc)
    return o_lat[0] + o_lat[1], o_ctx[0] + o_ctx[1]


def _diff_attn_probs(q, k, lam):
    s = jnp.einsum('bqhmd,bkhmd->bhmqk', q, k).astype(jnp.float32) * (A_DH ** -0.5)
    p = jax.nn.softmax(s, axis=-1)
    return p[:, :, 0] - lam * p[:, :, 1]


def _diff_attention_latent(q, k, v, kc, vc, lam):
    B, T, H = q.shape[:3]
    k_all = jnp.concatenate([kc, k], axis=1)
    v_all = jnp.concatenate([vc, v], axis=1)
    nb = T // Q_BLOCK
    qb = jnp.moveaxis(q.reshape(B, nb, Q_BLOCK, H, 2, A_DH), 1, 0)

    def block(qi):
        a = _diff_attn_probs(qi, k_all, lam)
        return jnp.einsum('bhqk,bkhe->bqhe', a.astype(v_all.dtype), v_all)

    o = lax.map(block, qb)
    return jnp.moveaxis(o, 0, 1).reshape(B, T, H, A_DV)


def _diff_attention_ctx(qc, kc, vc, lam):
    a = _diff_attn_probs(qc, kc, lam)
    return jnp.einsum('bhqk,bkhe->bqhe', a.astype(vc.dtype), vc)


def _window_attention_latent(q, k, v, kc, vc, sink):
    B, T, H, d = q.shape
    G = H // D_KV_HEADS
    L = kc.shape[1]
    nb = T // Q_BLOCK
    band = Q_BLOCK + 2 * D_WINDOW
    pad = ((0, 0), (D_WINDOW, D_WINDOW), (0, 0), (0, 0))
    kp, vp = jnp.pad(k, pad), jnp.pad(v, pad)
    qb = jnp.moveaxis(q.reshape(B, nb, Q_BLOCK, D_KV_HEADS, G, d), 1, 0)
    rel = jnp.arange(band)[None, :] - D_WINDOW - jnp.arange(Q_BLOCK)[:, None]
    in_window = jnp.abs(rel) <= D_WINDOW
    sink_l = sink.astype(jnp.float32).reshape(1, D_KV_HEADS, G, 1, 1)
    scale = d ** -0.5

    def block(args):
        i, qi = args
        start = i * Q_BLOCK
        kb = lax.dynamic_slice_in_dim(kp, start, band, axis=1)
        vb = lax.dynamic_slice_in_dim(vp, start, band, axis=1)
        key_pos = start - D_WINDOW + jnp.arange(band)
        mask = in_window & ((key_pos >= 0) & (key_pos < T))[None, :]
        s_band = jnp.einsum('bqkgd,bnkd->bkgqn', qi, kb).astype(jnp.float32) * scale
        s_band = jnp.where(mask, s_band, -jnp.inf)
        s_ctx = jnp.einsum('bqkgd,bnkd->bkgqn', qi, kc).astype(jnp.float32) * scale
        logits = jnp.concatenate([jnp.broadcast_to(sink_l, s_ctx.shape[:-1] + (1,)), s_ctx, s_band], axis=-1)
        p = jax.nn.softmax(logits, axis=-1).astype(v.dtype)
        return (jnp.einsum('bkgqn,bnkd->bqkgd', p[..., 1:1 + L], vc)
                + jnp.einsum('bkgqn,bnkd->bqkgd', p[..., 1 + L:], vb))

    o = lax.map(block, (jnp.arange(nb), qb))
    return jnp.moveaxis(o, 0, 1).reshape(B, T, H, d)


def _window_attention_ctx(qc, kc, vc, sink):
    B, L, H, d = qc.shape
    G = H // D_KV_HEADS
    s = jnp.einsum('bqkgd,bnkd->bkgqn', qc.reshape(B, L, D_KV_HEADS, G, d), kc).astype(jnp.float32) * d ** -0.5
    sink_l = jnp.broadcast_to(sink.astype(jnp.float32).reshape(1, D_KV_HEADS, G, 1, 1), s.shape[:-1] + (1,))
    p = jax.nn.softmax(jnp.concatenate([sink_l, s], axis=-1), axis=-1)[..., 1:]
    return jnp.einsum('bkgqn,bnkd->bqkgd', p.astype(vc.dtype), vc).reshape(B, L, H, d)


def _even_mixer(layer, h, hc, w_in, w_out, diff_lambda, subln_w, hgrn_lb, hgrn_norm_w, cos, sin, need_ctx):
    B, T, _ = h.shape
    L = hc.shape[1]
    q_a, k_a, v_a, q_b, i_b, ff_b, fb_b, g_b = _split_cols(h @ w_in, EVEN_SPLITS)
    qc_a, kc_a, vc_a, qc_b, ic_b, ffc_b, fbc_b, gc_b = _split_cols(hc @ w_in, EVEN_SPLITS)

    lambda_init = LAMBDA_INIT_BASE - LAMBDA_INIT_SPAN * math.exp(-LAMBDA_INIT_RATE * layer)
    lam_f = diff_lambda.astype(jnp.float32)
    lam = jnp.exp(jnp.sum(lam_f[0] * lam_f[1])) - jnp.exp(jnp.sum(lam_f[2] * lam_f[3])) + lambda_init

    def qk_latent(t):
        return _apply_rope(t.reshape(B, T, 2 * A_HEADS, A_DH), cos, sin).reshape(B, T, A_HEADS, 2, A_DH)

    kc = kc_a.reshape(B, L, A_HEADS, 2, A_DH)
    vc = vc_a.reshape(B, L, A_HEADS, A_DV)
    o_a = _diff_attention_latent(qk_latent(q_a), qk_latent(k_a), v_a.reshape(B, T, A_HEADS, A_DV), kc, vc, lam)
    o_a = (_rmsnorm(o_a, subln_w) * (1.0 - lambda_init)).astype(h.dtype)

    lb = jnp.cumsum(jax.nn.softmax(hgrn_lb.astype(jnp.float32), axis=0), axis=0)[layer]

    def hgrn_dirs(q, i, ff, fb, n):
        q = (jax.nn.silu(q.astype(jnp.float32)) * B_DK ** -0.5).reshape(B, n, B_HEADS, B_DK)
        i = i.astype(jnp.float32).reshape(B, n, B_HEADS, B_DV)
        out = []
        for f_logit in (ff, fb):
            f = (lb + (1.0 - lb) * jax.nn.sigmoid(f_logit.astype(jnp.float32))).reshape(B, n, B_HEADS, B_DK)
            out.append((q, 1.0 - f, i, jnp.log(f)))
        return out

    lat_f, lat_b = hgrn_dirs(q_b, i_b, ff_b, fb_b, T)
    ctx_f, ctx_b = hgrn_dirs(qc_b, ic_b, ffc_b, fbc_b, L)
    s0 = jnp.zeros((B, B_HEADS, B_DK, B_DV), jnp.float32)
    o_b, oc_b = _bidirectional_scan(_gla_chunked, lat_f, lat_b, ctx_f, ctx_b, s0)

    def hgrn_out(o, g, n):
        gate = jax.nn.silu(g.astype(jnp.float32).reshape(B, n, B_HEADS, B_DV))
        return (_rmsnorm(o, hgrn_norm_w) * gate).reshape(B, n, B_HEADS * B_DV)

    y = jnp.concatenate([o_a.reshape(B, T, A_HEADS * A_DV), hgrn_out(o_b, g_b, T).astype(h.dtype)], axis=-1) @ w_out
    if not need_ctx:
        return y, None
    oc_a = _rmsnorm(_diff_attention_ctx(qc_a.reshape(B, L, A_HEADS, 2, A_DH), kc, vc, lam), subln_w) * (1.0 - lambda_init)
    yc = jnp.concatenate([oc_a.reshape(B, L, A_HEADS * A_DV).astype(hc.dtype),
                          hgrn_out(oc_b, gc_b, L).astype(hc.dtype)], axis=-1) @ w_out
    return y, yc


def _odd_mixer(h, hc, w_in, w_out, conv_w, a_log, dt_bias, gdn_norm_w, sink, cos, sin, need_ctx):
    B, T, _ = h.shape
    L = hc.shape[1]
    qkv, z, gates, q_d, kv_d = _split_cols(h @ w_in, ODD_SPLITS)
    qkvc, zc, gatesc, qc_d, kvc_d = _split_cols(hc @ w_in, ODD_SPLITS)

    a_log_f = a_log.astype(jnp.float32)
    dt_f = dt_bias.astype(jnp.float32)

    def gdn_dirs(qkv_t, gates_t, n):
        qkv_t = jax.nn.silu(_centred_depthwise_conv(qkv_t, conv_w))
        q, k, v = _split_cols(qkv_t, (C_HEADS * C_DK, C_HEADS * C_DK, C_HEADS * C_DV))
        q = _l2norm(q.reshape(B, n, C_HEADS, C_DK)) * C_DK ** -0.5
        k = _l2norm(k.reshape(B, n, C_HEADS, C_DK))
        v = v.astype(jnp.float32).reshape(B, n, C_HEADS, C_DV)
        a_f, a_b, b_f, b_b = jnp.split(gates_t.astype(jnp.float32), 4, axis=-1)
        out = []
        for d, (a_t, b_t) in enumerate(((a_f, b_f), (a_b, b_b))):
            log_alpha = -jnp.exp(a_log_f[d]) * jax.nn.softplus(a_t + dt_f[d])
            out.append((q, k, v, log_alpha, jax.nn.sigmoid(b_t)))
        return out

    lat_f, lat_b = gdn_dirs(qkv, gates, T)
    ctx_f, ctx_b = gdn_dirs(qkvc, gatesc, L)
    s0 = jnp.zeros((B, C_HEADS, C_DK, C_DV), jnp.float32)
    o_c, oc_c = _bidirectional_scan(_gated_delta_chunked, lat_f, lat_b, ctx_f, ctx_b, s0)

    def gdn_out(o, zt, n):
        gate = jax.nn.silu(zt.astype(jnp.float32).reshape(B, n, C_HEADS, C_DV))
        return (_rmsnorm(o, gdn_norm_w) * gate).reshape(B, n, C_HEADS * C_DV)

    k_d, v_d = jnp.split(kv_d, 2, axis=-1)
    kc_d, vc_d = jnp.split(kvc_d, 2, axis=-1)
    q_lat = _apply_rope(q_d.reshape(B, T, D_HEADS, D_DH), cos, sin)
    k_lat = _apply_rope(k_d.reshape(B, T, D_KV_HEADS, D_DH), cos, sin)
    kc = kc_d.reshape(B, L, D_KV_HEADS, D_DH)
    vc = vc_d.reshape(B, L, D_KV_HEADS, D_DH)
    o_d = _window_attention_latent(q_lat, k_lat, v_d.reshape(B, T, D_KV_HEADS, D_DH), kc, vc, sink)

    y = jnp.concatenate([gdn_out(o_c, z, T).astype(h.dtype), o_d.reshape(B, T, D_HEADS * D_DH)], axis=-1) @ w_out
    if not need_ctx:
        return y, None
    oc_d = _window_attention_ctx(qc_d.reshape(B, L, D_HEADS, D_DH), kc, vc, sink)
    yc = jnp.concatenate([gdn_out(oc_c, zc, L).astype(hc.dtype), oc_d.reshape(B, L, D_HEADS * D_DH)], axis=-1) @ w_out
    return y, yc


def kernel(x, c, ctx, c_ctx, ada_w, ada_b, norm_w, final_norm_w, even_w_in, even_w_out, diff_lambda, diff_subln_w, hgrn_lb, hgrn_norm_w, odd_w_in, odd_w_out, gdn_conv_w, gdn_a_log, gdn_dt_bias, gdn_norm_w, swa_sink, peer_w_q, peer_keys, peer_u, peer_v):
    B, T, D = x.shape
    cos, sin = _axial_rope_tables(T // GRID_W)
    xc = ctx
    for layer in range(DEPTH):
        need_ctx = layer < DEPTH - 1
        w_ada, b_ada = ada_w[layer], ada_b[layer]
        sh1, sc1, g1, sh2, sc2, g2 = [m[:, None, :] for m in _ada_split(c, w_ada, b_ada)]
        sh1c, sc1c, g1c, sh2c, sc2c, g2c = _ada_split(c_ctx, w_ada, b_ada)
        h = _modulate(x, norm_w[layer, 0], sh1, sc1)
        hc = _modulate(xc, norm_w[layer, 0], sh1c, sc1c)
        if layer % 2 == 0:
            e = layer // 2
            y, yc = _even_mixer(layer, h, hc, even_w_in[e], even_w_out[e], diff_lambda[e], diff_subln_w[e],
                                hgrn_lb, hgrn_norm_w[e], cos, sin, need_ctx)
        else:
            o = layer // 2
            y, yc = _odd_mixer(h, hc, odd_w_in[o], odd_w_out[o], gdn_conv_w[o], gdn_a_log[o], gdn_dt_bias[o],
                               gdn_norm_w[o], swa_sink[o], cos, sin, need_ctx)
        wq_t = peer_w_q[layer].T
        tab = jnp.concatenate([peer_u[layer], peer_v[layer]], axis=-1).reshape(-1, ROW_TILES, LANES)
        x = x + g1 * y
        x = x + g2 * _peer_ffn(x, norm_w[layer, 1], sh2, sc2, wq_t, peer_keys[layer], tab)
        if need_ctx:
            xc = xc + g1c * yc
            bc = lambda v: jnp.broadcast_to(v.reshape(1, 1, D), (B, 1, D))
            xc = xc + g2c * _peer_ffn(xc, norm_w[layer, 1], bc(sh2c), bc(sc2c), wq_t, peer_keys[layer], tab)
    return _rmsnorm(x, final_norm_w)
```

```python
import functools
import math

import jax
import jax.numpy as jnp
import numpy as np
from jax import lax
from jax.experimental import pallas as pl
from jax.experimental.pallas import tpu as pltpu

D_MODEL = 1024
DEPTH = 2
GRID_W = 64
ADA_CHUNKS = 6
NORM_EPS = 1e-6
ROPE_THETA = 10000.0
ROPE_DIM = 64
ROPE_FREQS = ROPE_DIM // 4

A_HEADS = 4
A_DH = 64
A_DV = 2 * A_DH
LAMBDA_INIT_BASE = 0.8
LAMBDA_INIT_SPAN = 0.6
LAMBDA_INIT_RATE = 0.3
B_HEADS = 8
B_DK = 64
B_DV = 64
B_CHUNK = 32
C_HEADS = 4
C_DK = 128
C_DV = 128
C_CHUNK = 64
C_CONV = 5
D_HEADS = 8
D_KV_HEADS = 2
D_DH = 64
D_WINDOW = 128
P_HEADS = 8
P_NKEYS = 128
P_DQ = 256
P_TOPK = 16

SUBLANES = 8
LANES = 128
VMEM_LIMIT = 48 * 1024 * 1024

MXU_DT = jnp.bfloat16
HI = lax.Precision.HIGHEST

HALF = 512
EVEN_IN = 4096
ODD_IN_PAD = 2944
ODD_ROPE = (2048, 2688)
EVEN_ROPE = (0, 1024)

PROJ_TM = 256
ATT_TQ = 256
SWA_TQ = 128
PREP_TR = 256

P_NSEL = P_HEADS * P_TOPK
ROUTE_TN = 256
APPLY_TB = 8
ROW_TILES = 2 * D_MODEL // LANES
ROW_GROUPS = P_NSEL // SUBLANES


def _cparams(n_axes=1):
    return pltpu.CompilerParams(dimension_semantics=("arbitrary",) * n_axes, vmem_limit_bytes=VMEM_LIMIT)


def _mm(a, b):
    return jnp.dot(a.astype(MXU_DT), b.astype(MXU_DT), preferred_element_type=jnp.float32)


def _mm_nt(a, b):
    return lax.dot_general(a.astype(MXU_DT), b.astype(MXU_DT), (((1,), (1,)), ((), ())),
                           preferred_element_type=jnp.float32)


def _mm_tn(a, b):
    return lax.dot_general(a.astype(MXU_DT), b.astype(MXU_DT), (((0,), (0,)), ((), ())),
                           preferred_element_type=jnp.float32)


def _mm_hi(a, b):
    return jnp.dot(a, b, precision=HI, preferred_element_type=jnp.float32)


def _silu(x):
    return x * jax.nn.sigmoid(x)


def _ada_kernel(c_ref, w_ref, b_ref, o_ref):
    o_ref[...] = _mm(_silu(c_ref[...]), w_ref[...]) + b_ref[...]


def _ada_linear(cvecs, w, b):
    rows = cvecs.shape[0]
    return pl.pallas_call(
        _ada_kernel,
        grid=(ADA_CHUNKS,),
        in_specs=[pl.BlockSpec((rows, D_MODEL), lambda j: (0, 0)),
                  pl.BlockSpec((D_MODEL, D_MODEL), lambda j: (0, j)),
                  pl.BlockSpec((1, D_MODEL), lambda j: (0, j))],
        out_specs=pl.BlockSpec((rows, D_MODEL), lambda j: (0, j)),
        out_shape=jax.ShapeDtypeStruct((rows, ADA_CHUNKS * D_MODEL), jnp.float32),
        compiler_params=_cparams(), name="ada_linear",
    )(cvecs, w, b.reshape(1, -1))


def _inproj_kernel(x_ref, sh_ref, sc_ref, nw_ref, w_ref, *rest, rope):
    o_ref = rest[-1]
    x = x_ref[...]
    h = x * lax.rsqrt(jnp.mean(x * x, axis=-1, keepdims=True) + NORM_EPS) * nw_ref[...]
    h = h * (1.0 + sc_ref[0]) + sh_ref[0]
    y = jnp.dot(h.astype(MXU_DT), w_ref[...], preferred_element_type=jnp.float32)
    o_ref[...] = y
    if rope is not None:
        cos, sin_up, sin_dn = rest[0][...], rest[1][...], rest[2][...]
        for c in range(rope[0] // LANES, rope[1] // LANES):
            t = y[:, c * LANES:(c + 1) * LANES]
            o_ref[:, c * LANES:(c + 1) * LANES] = (t * cos + pltpu.roll(t, LANES - 16, 1) * sin_up
                                                   + pltpu.roll(t, 16, 1) * sin_dn)


def _inproj(x2, shift, scale, norm_w, w, tokens_per_batch, rope=None, rope_tabs=None):
    n_tok, f_out = x2.shape[0], w.shape[1]
    tpb = tokens_per_batch // PROJ_TM
    in_specs = [pl.BlockSpec((PROJ_TM, D_MODEL), lambda i: (i, 0)),
                pl.BlockSpec((1, 1, D_MODEL), lambda i: (i // tpb, 0, 0)),
                pl.BlockSpec((1, 1, D_MODEL), lambda i: (i // tpb, 0, 0)),
                pl.BlockSpec((1, D_MODEL), lambda i: (0, 0)),
                pl.BlockSpec((D_MODEL, f_out), lambda i: (0, 0))]
    args = [x2, shift, scale, norm_w.reshape(1, D_MODEL), w]
    if rope is not None:
        in_specs += [pl.BlockSpec((PROJ_TM, LANES), lambda i: (i % tpb, 0))] * 3
        args += list(rope_tabs)
    return pl.pallas_call(
        functools.partial(_inproj_kernel, rope=rope),
        grid=(n_tok // PROJ_TM,),
        in_specs=in_specs,
        out_specs=pl.BlockSpec((PROJ_TM, f_out), lambda i: (i, 0)),
        out_shape=jax.ShapeDtypeStruct((n_tok, f_out), jnp.float32),
        compiler_params=_cparams(), name="inproj",
    )(*args)


def _rope_tables(seq):
    rows = seq // GRID_W
    row = jnp.repeat(jnp.arange(rows, dtype=jnp.float32), GRID_W)
    col = jnp.tile(jnp.arange(GRID_W, dtype=jnp.float32), rows)
    inv = ROPE_THETA ** (-jnp.arange(ROPE_FREQS, dtype=jnp.float32) / ROPE_FREQS)
    ar = row[:, None] * inv[None, :]
    ac = col[:, None] * inv[None, :]
    ang = jnp.concatenate([ar, ar, ac, ac], axis=-1)
    cos = jnp.tile(jnp.cos(ang), (1, 2))
    sin = jnp.tile(jnp.sin(ang), (1, 2))
    quarter = (jnp.arange(LANES) % ROPE_DIM) // ROPE_FREQS
    sin_up = jnp.where(quarter % 2 == 0, -sin, 0.0)
    sin_dn = jnp.where(quarter % 2 == 1, sin, 0.0)
    return cos, sin_up, sin_dn


def _diffattn_kernel(lam_ref, q_ref, kc_ref, vc_ref, *rest, has_lat, out_scale):
    if has_lat:
        kl_ref, vl_ref, sw_ref, o_ref = rest
    else:
        sw_ref, o_ref = rest
    lam = lam_ref[0]
    q = q_ref[...] * (A_DH ** -0.5)
    lane = lax.broadcasted_iota(jnp.int32, (1, LANES), 1)
    first = lane < A_DH
    kc = kc_ref[...]
    probs = []
    for m in range(2):
        qm = jnp.where(first if m == 0 else jnp.logical_not(first), q, 0.0)
        s_c = _mm_nt(qm, kc)
        mx = jnp.max(s_c, axis=-1, keepdims=True)
        if has_lat:
            s_l = _mm_nt(qm, kl_ref[...])
            mx = jnp.maximum(mx, jnp.max(s_l, axis=-1, keepdims=True))
            e_l = jnp.exp(s_l - mx)
        e_c = jnp.exp(s_c - mx)
        den = jnp.sum(e_c, axis=-1, keepdims=True)
        if has_lat:
            den = den + jnp.sum(e_l, axis=-1, keepdims=True)
        probs.append((e_c / den, (e_l / den) if has_lat else None))
    a_c = probs[0][0] - lam * probs[1][0]
    o = _mm(a_c, vc_ref[...])
    if has_lat:
        a_l = probs[0][1] - lam * probs[1][1]
        o = o + _mm(a_l, vl_ref[...])
    o = o * lax.rsqrt(jnp.mean(o * o, axis=-1, keepdims=True) + NORM_EPS) * sw_ref[...]
    o_ref[...] = o * out_scale


def _diff_attention(lam, pq, pc, pl_lat, subln_w, n_q, n_ctx, n_lat, out_scale):
    batch = pq.shape[0] // n_q
    tq = min(ATT_TQ, n_q)
    nq = n_q // tq
    has_lat = pl_lat is not None
    in_specs = [pl.BlockSpec(memory_space=pltpu.SMEM),
                pl.BlockSpec((tq, LANES), lambda b, h, i: (b * nq + i, h)),
                pl.BlockSpec((n_ctx, LANES), lambda b, h, i: (b, A_HEADS + h)),
                pl.BlockSpec((n_ctx, LANES), lambda b, h, i: (b, 2 * A_HEADS + h))]
    args = [lam, pq, pc, pc]
    if has_lat:
        in_specs += [pl.BlockSpec((n_lat, LANES), lambda b, h, i: (b, A_HEADS + h)),
                     pl.BlockSpec((n_lat, LANES), lambda b, h, i: (b, 2 * A_HEADS + h))]
        args += [pl_lat, pl_lat]
    in_specs.append(pl.BlockSpec((1, LANES), lambda b, h, i: (0, 0)))
    args.append(subln_w.reshape(1, A_DV))
    return pl.pallas_call(
        functools.partial(_diffattn_kernel, has_lat=has_lat, out_scale=out_scale),
        grid=(batch, A_HEADS, nq),
        in_specs=in_specs,
        out_specs=pl.BlockSpec((tq, LANES), lambda b, h, i: (b * nq + i, h)),
        out_shape=jax.ShapeDtypeStruct((pq.shape[0], HALF), jnp.float32),
        compiler_params=_cparams(3), name="diff_attn",
    )(*args)


def _tri(n, upper):
    r = lax.broadcasted_iota(jnp.int32, (n, n), 0)
    c = lax.broadcasted_iota(jnp.int32, (n, n), 1)
    return (c >= r) if upper else (c <= r)


def _hgrn_dir(q_ref, i_ref, f_ref, lb, s_ref, o_ref, rev):
    n = B_CHUNK
    incl = _tri(n, rev)
    f = lb + (1.0 - lb) * jax.nn.sigmoid(f_ref[...])
    k = 1.0 - f
    g = jnp.log(f)
    qh = _silu(q_ref[...]) * (B_DK ** -0.5)
    v = i_ref[...]
    b = _mm_hi(incl.astype(jnp.float32), g)
    mid = n // 2
    ref_row = (n - 1 - mid) if rev else mid
    last_row = 0 if rev else n - 1
    ref = b[ref_row:ref_row + 1, :]
    b_last = b[last_row:last_row + 1, :]
    q_in = qh * jnp.exp(b - ref)
    k_in = k * jnp.exp(ref - b)
    q_st = qh * jnp.exp(b)
    k_st = k * jnp.exp(b_last - b)
    decay = jnp.exp(b_last)
    outs = []
    for h in range(B_HEADS):
        sl = slice(h * B_DK, (h + 1) * B_DK)
        st = s_ref[0, h]
        att = jnp.where(incl, _mm_nt(q_in[:, sl], k_in[:, sl]), 0.0)
        outs.append(_mm(att, v[:, sl]) + _mm_nt(q_st[:, sl], st))
        s_ref[0, h] = st * decay[:, sl] + _mm_tn(v[:, sl], k_st[:, sl])
    o_ref[...] = jnp.concatenate(outs, axis=-1)


def _hgrn_kernel(qf, if_, ff, qb, ib, fb, lb_ref, s0f, s0b, of, ob, sf, sb):
    @pl.when(pl.program_id(1) == 0)
    def _():
        sf[...] = s0f[...]
        sb[...] = s0b[...]
    lb = lb_ref[...]
    _hgrn_dir(qf, if_, ff, lb, sf, of, False)
    _hgrn_dir(qb, ib, fb, lb, sb, ob, True)


def _hgrn_scan(p, lb, s0f, s0b, n):
    batch = p.shape[0] // n
    nc = n // B_CHUNK
    blk = lambda col, rev: pl.BlockSpec(
        (B_CHUNK, HALF), (lambda b, c: (b * nc + nc - 1 - c, col)) if rev else (lambda b, c: (b * nc + c, col)))
    st = pl.BlockSpec((1, B_HEADS, B_DK, B_DV), lambda b, c: (b, 0, 0, 0))
    o_sd = jax.ShapeDtypeStruct((p.shape[0], HALF), jnp.float32)
    s_sd = jax.ShapeDtypeStruct((batch, B_HEADS, B_DK, B_DV), jnp.float32)
    return pl.pallas_call(
        _hgrn_kernel,
        grid=(batch, nc),
        in_specs=[blk(3, False), blk(4, False), blk(5, False), blk(3, True), blk(4, True), blk(6, True),
                  pl.BlockSpec((1, HALF), lambda b, c: (0, 0)), st, st],
        out_specs=[blk(0, False), blk(0, True), st, st],
        out_shape=[o_sd, o_sd, s_sd, s_sd],
        compiler_params=_cparams(2), name="hgrn_scan",
    )(p, p, p, p, p, p, lb.reshape(1, HALF), s0f, s0b)


def _gdn_prep_kernel(x_ref, prev_ref, next_ref, w_ref, o_ref, *, tiles_per_seq):
    i = pl.program_id(0)
    first = (i % tiles_per_seq) == 0
    last = (i % tiles_per_seq) == tiles_per_seq - 1
    x = x_ref[...]
    prev = jnp.where(first, 0.0, prev_ref[...])
    nxt = jnp.where(last, 0.0, next_ref[...])
    xe = jnp.concatenate([prev, x, nxt], axis=0)
    pad = C_CONV // 2
    acc = jnp.zeros_like(x)
    for j in range(C_CONV):
        acc = acc + xe[SUBLANES + j - pad:SUBLANES + j - pad + PREP_TR, :] * w_ref[j:j + 1, :]
    y = _silu(acc)
    for c in range(3 * C_HEADS):
        t = y[:, c * LANES:(c + 1) * LANES]
        if c < 2 * C_HEADS:
            t = t * lax.rsqrt(jnp.sum(t * t, axis=-1, keepdims=True) + NORM_EPS)
            if c < C_HEADS:
                t = t * (C_DK ** -0.5)
        o_ref[:, c * LANES:(c + 1) * LANES] = t


def _gdn_prep(p, conv_w, n):
    width = 3 * HALF
    nt = p.shape[0] // PREP_TR
    tps = n // PREP_TR
    rb = PREP_TR // SUBLANES
    last_blk = p.shape[0] // SUBLANES - 1
    return pl.pallas_call(
        functools.partial(_gdn_prep_kernel, tiles_per_seq=tps),
        grid=(nt,),
        in_specs=[pl.BlockSpec((PREP_TR, width), lambda i: (i, 0)),
                  pl.BlockSpec((SUBLANES, width), lambda i: (jnp.maximum(i * rb - 1, 0), 0)),
                  pl.BlockSpec((SUBLANES, width), lambda i: (jnp.minimum((i + 1) * rb, last_blk), 0)),
                  pl.BlockSpec((SUBLANES, width), lambda i: (0, 0))],
        out_specs=pl.BlockSpec((PREP_TR, width), lambda i: (i, 0)),
        out_shape=jax.ShapeDtypeStruct((p.shape[0], width), jnp.float32),
        compiler_params=_cparams(), name="gdn_prep",
    )(p, p, p, jnp.pad(conv_w, ((0, SUBLANES - C_CONV), (0, 0))))


def _gdn_dir(q_ref, k_ref, v_ref, gt_ref, nea, dtv, s_ref, o_ref, d, rev):
    n = C_CHUNK
    incl = _tri(n, rev)
    r = lax.broadcasted_iota(jnp.int32, (n, n), 0)
    c = lax.broadcasted_iota(jnp.int32, (n, n), 1)
    strict = (c > r) if rev else (c < r)
    eye = (r == c).astype(jnp.float32)
    gates = gt_ref[...]
    x = gates + dtv
    la = nea * (jnp.maximum(x, 0.0) + jnp.log1p(jnp.exp(-jnp.abs(x))))
    beta_all = jax.nn.sigmoid(gates)
    inclf = incl.astype(jnp.float32)
    g_cols = _mm_hi(inclf, la)
    g_rows = lax.dot_general(la, inclf, (((0,), (1,)), ((), ())), precision=HI,
                             preferred_element_type=jnp.float32)
    last_row = 0 if rev else n - 1
    outs = []
    for h in range(C_HEADS):
        col = d * C_HEADS + h
        sl = slice(h * LANES, (h + 1) * LANES)
        gc = g_cols[:, col:col + 1]
        gr = g_rows[col:col + 1, :]
        beta = beta_all[:, 2 * C_HEADS + col:2 * C_HEADS + col + 1]
        q, k, v = q_ref[:, sl], k_ref[:, sl], v_ref[:, sl]
        S = s_ref[0, h]
        decay = jnp.where(incl, jnp.exp(jnp.where(incl, gc - gr, 0.0)), 0.0)
        kb = k * beta
        nmat = jnp.where(strict, _mm_nt(kb, k) * decay, 0.0)
        inv = eye - nmat
        pw = _mm_hi(nmat, nmat)
        for step in range(5):
            inv = inv + _mm_hi(inv, pw)
            if step < 4:
                pw = _mm_hi(pw, pw)
        rhs = jnp.concatenate([v * beta, kb * jnp.exp(gc)], axis=-1)
        sol = _mm_hi(inv, rhs)
        v_new = sol[:, :C_DV] - _mm(sol[:, C_DV:], S)
        att = _mm_nt(q, k) * decay
        outs.append(_mm(q * jnp.exp(gc), S) + _mm(att, v_new))
        g_last = gc[last_row:last_row + 1, :]
        s_ref[0, h] = S * jnp.exp(g_last) + _mm_tn(k * jnp.exp(g_last - gc), v_new)
    o_ref[...] = jnp.concatenate(outs, axis=-1)


def _gdn_kernel(qf, kf, vf, gf, qb, kb, vb, gb, nea_ref, dt_ref, s0f, s0b, of, ob, sf, sb):
    @pl.when(pl.program_id(1) == 0)
    def _():
        sf[...] = s0f[...]
        sb[...] = s0b[...]
    nea, dtv = nea_ref[...], dt_ref[...]
    _gdn_dir(qf, kf, vf, gf, nea, dtv, sf, of, 0, False)
    _gdn_dir(qb, kb, vb, gb, nea, dtv, sb, ob, 1, True)


def _gdn_scan(qkv, p, nea, dtv, s0f, s0b, n):
    batch = p.shape[0] // n
    nc = n // C_CHUNK
    gate_col = (ODD_IN_PAD - LANES) // LANES

    def blk(width, col, rev):
        return pl.BlockSpec((C_CHUNK, width),
                            (lambda b, c: (b * nc + nc - 1 - c, col)) if rev else (lambda b, c: (b * nc + c, col)))
    st = pl.BlockSpec((1, C_HEADS, C_DK, C_DV), lambda b, c: (b, 0, 0, 0))
    vec = pl.BlockSpec((1, LANES), lambda b, c: (0, 0))
    o_sd = jax.ShapeDtypeStruct((p.shape[0], HALF), jnp.float32)
    s_sd = jax.ShapeDtypeStruct((batch, C_HEADS, C_DK, C_DV), jnp.float32)
    return pl.pallas_call(
        _gdn_kernel,
        grid=(batch, nc),
        in_specs=[blk(HALF, 0, False), blk(HALF, 1, False), blk(HALF, 2, False), blk(LANES, gate_col, False),
                  blk(HALF, 0, True), blk(HALF, 1, True), blk(HALF, 2, True), blk(LANES, gate_col, True),
                  vec, vec, st, st],
        out_specs=[blk(HALF, 0, False), blk(HALF, 0, True), st, st],
        out_shape=[o_sd, o_sd, s_sd, s_sd],
        compiler_params=_cparams(2), name="gdn_scan",
    )(qkv, qkv, qkv, p, qkv, qkv, qkv, p, nea, dtv, s0f, s0b)


def _swa_kernel(sink_ref, q_ref, kp_ref, k0_ref, kn_ref, vp_ref, v0_ref, vn_ref, kc_ref, vc_ref, o_ref, *, nblk):
    i = pl.program_id(1)
    tq = SWA_TQ
    q = q_ref[...] * (D_DH ** -0.5)
    kband = jnp.concatenate([kp_ref[...], k0_ref[...], kn_ref[...]], axis=0)
    vband = jnp.concatenate([vp_ref[...], v0_ref[...], vn_ref[...]], axis=0)
    qpos = lax.broadcasted_iota(jnp.int32, (tq, 3 * tq), 0)
    kpos = lax.broadcasted_iota(jnp.int32, (tq, 3 * tq), 1) - tq
    ok = jnp.abs(kpos - qpos) <= D_WINDOW
    ok = jnp.logical_and(ok, jnp.logical_or(kpos >= 0, i > 0))
    ok = jnp.logical_and(ok, jnp.logical_or(kpos < tq, i < nblk - 1))
    kc, vc = kc_ref[...], vc_ref[...]
    group = D_HEADS // D_KV_HEADS
    outs = []
    for h in range(D_HEADS):
        ks = slice((h // group) * D_DH, (h // group + 1) * D_DH)
        qh = q[:, h * D_DH:(h + 1) * D_DH]
        s_b = jnp.where(ok, _mm_nt(qh, kband[:, ks]), -jnp.inf)
        s_c = _mm_nt(qh, kc[:, ks])
        sink = sink_ref[h]
        mx = jnp.maximum(jnp.maximum(jnp.max(s_b, axis=-1, keepdims=True), jnp.max(s_c, axis=-1, keepdims=True)), sink)
        e_b = jnp.exp(s_b - mx)
        e_c = jnp.exp(s_c - mx)
        den = jnp.exp(sink - mx) + jnp.sum(e_b, axis=-1, keepdims=True) + jnp.sum(e_c, axis=-1, keepdims=True)
        outs.append(_mm(e_c / den, vc[:, ks]) + _mm(e_b / den, vband[:, ks]))
    o_ref[...] = jnp.concatenate(outs, axis=-1)


def _window_attention(sink, p, pc, n_lat, n_ctx):
    batch = p.shape[0] // n_lat
    nblk = n_lat // SWA_TQ
    qcol = ODD_ROPE[0] // HALF
    kcol = (ODD_ROPE[0] + HALF) // LANES
    vcol = kcol + 1
    prev = lambda col: pl.BlockSpec((SWA_TQ, LANES), lambda b, i: (b * nblk + jnp.maximum(i - 1, 0), col))
    cur = lambda col: pl.BlockSpec((SWA_TQ, LANES), lambda b, i: (b * nblk + i, col))
    nxt = lambda col: pl.BlockSpec((SWA_TQ, LANES), lambda b, i: (b * nblk + jnp.minimum(i + 1, nblk - 1), col))
    return pl.pallas_call(
        functools.partial(_swa_kernel, nblk=nblk),
        grid=(batch, nblk),
        in_specs=[pl.BlockSpec(memory_space=pltpu.SMEM),
                  pl.BlockSpec((SWA_TQ, HALF), lambda b, i: (b * nblk + i, qcol)),
                  prev(kcol), cur(kcol), nxt(kcol), prev(vcol), cur(vcol), nxt(vcol),
                  pl.BlockSpec((n_ctx, LANES), lambda b, i: (b, kcol)),
                  pl.BlockSpec((n_ctx, LANES), lambda b, i: (b, vcol))],
        out_specs=pl.BlockSpec((SWA_TQ, HALF), lambda b, i: (b * nblk + i, 0)),
        out_shape=jax.ShapeDtypeStruct((p.shape[0], HALF), jnp.float32),
        compiler_params=_cparams(2), name="window_attn",
    )(sink, p, p, p, p, p, p, p, pc, pc)


def _outproj_kernel(att_ref, of_ref, ob_ref, g_ref, nw_ref, w_ref, x_ref, gate_ref, o_ref, *, group, rec_first):
    o = of_ref[...] + ob_ref[...]
    sq = o * o
    if group == LANES:
        parts = [jnp.broadcast_to(jnp.mean(sq[:, c * LANES:(c + 1) * LANES], axis=-1, keepdims=True),
                                  (sq.shape[0], LANES)) for c in range(HALF // LANES)]
        ms = jnp.concatenate(parts, axis=-1)
    else:
        r = lax.broadcasted_iota(jnp.int32, (HALF, HALF), 0) // group
        c = lax.broadcasted_iota(jnp.int32, (HALF, HALF), 1) // group
        ms = _mm_hi(sq, (r == c).astype(jnp.float32)) * (1.0 / group)
    rec = o * lax.rsqrt(ms + NORM_EPS) * nw_ref[...] * _silu(g_ref[...])
    att = att_ref[...]
    top, bot = (rec, att) if rec_first else (att, rec)
    y = _mm(top, w_ref[:HALF, :]) + _mm(bot, w_ref[HALF:, :])
    o_ref[...] = x_ref[...] + gate_ref[0] * y


def _outproj(att, of, ob, p, gate_col, norm_w, w_out, x2, gate, tokens_per_batch, group, rec_first):
    n_tok = x2.shape[0]
    tpb = tokens_per_batch // PROJ_TM
    row = lambda col: pl.BlockSpec((PROJ_TM, HALF), lambda i: (i, col))
    return pl.pallas_call(
        functools.partial(_outproj_kernel, group=group, rec_first=rec_first),
        grid=(n_tok // PROJ_TM,),
        in_specs=[row(0), row(0), row(0), row(gate_col),
                  pl.BlockSpec((1, HALF), lambda i: (0, 0)),
                  pl.BlockSpec((2 * HALF, D_MODEL), lambda i: (0, 0)),
                  pl.BlockSpec((PROJ_TM, D_MODEL), lambda i: (i, 0)),
                  pl.BlockSpec((1, 1, D_MODEL), lambda i: (i // tpb, 0, 0))],
        out_specs=pl.BlockSpec((PROJ_TM, D_MODEL), lambda i: (i, 0)),
        out_shape=jax.ShapeDtypeStruct((n_tok, D_MODEL), jnp.float32),
        compiler_params=_cparams(), name="outproj",
    )(att, of, ob, p, norm_w.reshape(1, HALF), w_out, x2, gate)


def _top16_rows(s, payload=None):
    rows = s.shape[0]
    iota = lax.broadcasted_iota(jnp.int32, s.shape, 0)
    vals, poss, pays = [], [], []
    for _ in range(P_TOPK):
        m = jnp.max(s, axis=0, keepdims=True)
        pos = jnp.min(jnp.where(s == m, iota, rows), axis=0, keepdims=True)
        hit = iota == pos
        if payload is not None:
            pays.append(jnp.max(jnp.where(hit, payload, -1), axis=0, keepdims=True))
        vals.append(m)
        poss.append(pos)
        s = jnp.where(hit, -jnp.inf, s)
    cat = lambda xs: jnp.concatenate(xs, axis=0)
    return cat(vals), cat(poss), (cat(pays) if payload is not None else None)


def _route_kernel(x_ref, sh_ref, sc_ref, nw_ref, wqt_ref, keys_ref, h_ref, idx_ref, gate_ref):
    x = x_ref[...]
    h = x * lax.rsqrt(jnp.mean(x * x, axis=-1, keepdims=True) + NORM_EPS) * nw_ref[...]
    h = h * (1.0 + sc_ref[0]) + sh_ref[0]
    h_ref[...] = h
    qt = lax.dot_general(wqt_ref[...], h, (((1,), (1,)), ((), ())), precision=HI,
                         preferred_element_type=jnp.float32)
    idx_rows, gate_rows = [], []
    for hh in range(P_HEADS):
        tops = []
        for p in range(2):
            j = hh * 2 + p
            s = _mm_hi(keys_ref[j], qt[j * P_NKEYS:(j + 1) * P_NKEYS, :])
            v, i, _ = _top16_rows(s)
            tops.append((v, i))
        (v0, i0), (v1, i1) = tops
        cand = jnp.concatenate([v0[a:a + 1, :] + v1 for a in range(P_TOPK)], axis=0)
        cidx = jnp.concatenate([i0[a:a + 1, :] * P_NKEYS + i1 for a in range(P_TOPK)], axis=0)
        best, _, sel = _top16_rows(cand, cidx)
        e = jnp.exp(best - best[0:1, :])
        gate_rows.append(e / jnp.sum(e, axis=0, keepdims=True))
        idx_rows.append(sel)
    idx_t = jnp.concatenate(idx_rows, axis=0)
    gate_t = jnp.concatenate(gate_rows, axis=0)
    idx_ref[...] = idx_t.T
    for j in range(ROUTE_TN // APPLY_TB):
        gate_ref[j] = gate_t[:, j * APPLY_TB:(j + 1) * APPLY_TB]


def _peer_route(x2, shift, scale, norm_w, wq_t, keys, tokens_per_batch):
    n_tok = x2.shape[0]
    tiles_per_batch = tokens_per_batch // ROUTE_TN
    nkeys2 = 2 * P_HEADS
    return pl.pallas_call(
        _route_kernel,
        grid=(n_tok // ROUTE_TN,),
        in_specs=[
            pl.BlockSpec((ROUTE_TN, D_MODEL), lambda i: (i, 0)),
            pl.BlockSpec((1, 1, D_MODEL), lambda i: (i // tiles_per_batch, 0, 0)),
            pl.BlockSpec((1, 1, D_MODEL), lambda i: (i // tiles_per_batch, 0, 0)),
            pl.BlockSpec((1, D_MODEL), lambda i: (0, 0)),
            pl.BlockSpec((P_HEADS * P_DQ, D_MODEL), lambda i: (0, 0)),
            pl.BlockSpec((nkeys2, P_NKEYS, P_DQ // 2), lambda i: (0, 0, 0)),
        ],
        out_specs=[
            pl.BlockSpec((ROUTE_TN, D_MODEL), lambda i: (i, 0)),
            pl.BlockSpec((ROUTE_TN, P_NSEL), lambda i: (i, 0)),
            pl.BlockSpec((ROUTE_TN // APPLY_TB, P_NSEL, APPLY_TB), lambda i: (i, 0, 0)),
        ],
        out_shape=[
            jax.ShapeDtypeStruct((n_tok, D_MODEL), jnp.float32),
            jax.ShapeDtypeStruct((n_tok, P_NSEL), jnp.int32),
            jax.ShapeDtypeStruct((n_tok // APPLY_TB, P_NSEL, APPLY_TB), jnp.float32),
        ],
        compiler_params=_cparams(), name="peer_route",
    )(x2, shift, scale, norm_w.reshape(1, D_MODEL), wq_t, keys.reshape(nkeys2, P_NKEYS, P_DQ // 2))


def _apply_kernel(idx_cur, idx_nxt, h_ref, gate_ref, x_ref, g2_ref, tab_ref, o_ref, buf, sem):
    i = pl.program_id(0)
    n = pl.num_programs(0)
    slot = i % 2

    def row_copy(idx_ref, s, t, r):
        return pltpu.make_async_copy(
            tab_ref.at[idx_ref[t, r]],
            buf.at[s, t * ROW_GROUPS + r // SUBLANES, :, r % SUBLANES, :],
            sem.at[s])

    def issue(idx_ref, s):
        def body(t, carry):
            for r in range(P_NSEL):
                row_copy(idx_ref, s, t, r).start(priority=r % 2)
            return carry
        lax.fori_loop(0, APPLY_TB, body, 0)

    @pl.when(i == 0)
    def _():
        issue(idx_cur, 0)

    @pl.when(i + 1 < n)
    def _():
        issue(idx_nxt, 1 - slot)

    pltpu.make_async_copy(buf.at[slot], buf.at[slot], sem.at[slot]).wait()

    half = ROW_TILES // 2
    lane = lax.broadcasted_iota(jnp.int32, (P_NSEL, APPLY_TB), 1)
    a = jnp.zeros((P_NSEL, APPLY_TB), jnp.float32)
    for t in range(APPLY_TB):
        u = buf[slot, t * ROW_GROUPS:(t + 1) * ROW_GROUPS, :half]
        xt = h_ref[t:t + 1, :].reshape(1, half, 1, LANES)
        s = jnp.sum(u * xt, axis=1)
        at = jnp.sum(s, axis=-1, keepdims=True).reshape(P_NSEL, 1)
        a = jnp.where(lane == t, at, a)
    coef = gate_ref[0] * (0.5 * a * (1.0 + lax.erf(a * (2.0 ** -0.5))))
    for t in range(APPLY_TB):
        v = buf[slot, t * ROW_GROUPS:(t + 1) * ROW_GROUPS, half:]
        ct = coef[:, t:t + 1].reshape(ROW_GROUPS, 1, SUBLANES, 1)
        o = jnp.sum(jnp.sum(ct * v, axis=0), axis=1)
        o_ref[t:t + 1, :] = x_ref[t:t + 1, :] + g2_ref[0] * o.reshape(1, D_MODEL)


def _peer_apply(idx, h, gate_t, x2, g2, tab, tokens_per_batch):
    n_tok = h.shape[0]
    nt = n_tok // APPLY_TB
    tpb = tokens_per_batch // APPLY_TB
    return pl.pallas_call(
        _apply_kernel,
        grid=(nt,),
        in_specs=[
            pl.BlockSpec((APPLY_TB, P_NSEL), lambda i: (i, 0), memory_space=pltpu.SMEM),
            pl.BlockSpec((APPLY_TB, P_NSEL), lambda i: (jnp.minimum(i + 1, nt - 1), 0), memory_space=pltpu.SMEM),
            pl.BlockSpec((APPLY_TB, D_MODEL), lambda i: (i, 0)),
            pl.BlockSpec((1, P_NSEL, APPLY_TB), lambda i: (i, 0, 0)),
            pl.BlockSpec((APPLY_TB, D_MODEL), lambda i: (i, 0)),
            pl.BlockSpec((1, 1, D_MODEL), lambda i: (i // tpb, 0, 0)),
            pl.BlockSpec(memory_space=pl.ANY),
        ],
        out_specs=pl.BlockSpec((APPLY_TB, D_MODEL), lambda i: (i, 0)),
        out_shape=jax.ShapeDtypeStruct((n_tok, D_MODEL), jnp.float32),
        scratch_shapes=[pltpu.VMEM((2, APPLY_TB * ROW_GROUPS, ROW_TILES, SUBLANES, LANES), jnp.float32),
                        pltpu.SemaphoreType.DMA((2,))],
        compiler_params=_cparams(), name="peer_apply",
    )(idx, idx, h, gate_t, x2, g2, tab)


def _peer_block(x2, norm_w, shift, scale, g2, wq_t, keys, tab, tokens_per_batch):
    h, idx, gate_t = _peer_route(x2, shift, scale, norm_w, wq_t, keys, tokens_per_batch)
    return _peer_apply(idx, h, gate_t, x2, g2, tab, tokens_per_batch)


def _final_norm_kernel(x_ref, w_ref, o_ref):
    x = x_ref[...]
    o_ref[...] = x * lax.rsqrt(jnp.mean(x * x, axis=-1, keepdims=True) + NORM_EPS) * w_ref[...]


def _final_norm(x2, w):
    tm = 512
    return pl.pallas_call(
        _final_norm_kernel,
        grid=(x2.shape[0] // tm,),
        in_specs=[pl.BlockSpec((tm, D_MODEL), lambda i: (i, 0)), pl.BlockSpec((1, D_MODEL), lambda i: (0, 0))],
        out_specs=pl.BlockSpec((tm, D_MODEL), lambda i: (i, 0)),
        out_shape=jax.ShapeDtypeStruct(x2.shape, jnp.float32),
        compiler_params=_cparams(), name="final_norm",
    )(x2, w.reshape(1, D_MODEL))


def _even_layer(layer, x2, xc2, mods, modc, norm_w, w_in, w_out, diff_lambda, subln_w, hgrn_lb, hgrn_norm_w,
                rope_tabs, T, L, need_ctx):
    B = x2.shape[0] // T
    sh1, sc1, g1 = mods
    sh1c, sc1c, g1c = modc
    w_in_b = w_in.astype(MXU_DT)
    p = _inproj(x2, sh1, sc1, norm_w, w_in_b, T, rope=EVEN_ROPE, rope_tabs=rope_tabs)
    pc = _inproj(xc2, sh1c, sc1c, norm_w, w_in_b, L)

    lambda_init = LAMBDA_INIT_BASE - LAMBDA_INIT_SPAN * math.exp(-LAMBDA_INIT_RATE * layer)
    lam = (jnp.exp(jnp.sum(diff_lambda[0] * diff_lambda[1])) - jnp.exp(jnp.sum(diff_lambda[2] * diff_lambda[3]))
           + lambda_init).reshape(1)
    o_a = _diff_attention(lam, p, pc, p, subln_w, T, L, T, 1.0 - lambda_init)

    lb = jnp.cumsum(jax.nn.softmax(hgrn_lb, axis=0), axis=0)[layer]
    s0 = jnp.zeros((B, B_HEADS, B_DK, B_DV), jnp.float32)
    oc_f, oc_b, sf, sb = _hgrn_scan(pc, lb, s0, s0, L)
    o_f, o_b, _, _ = _hgrn_scan(p, lb, sf, sb, T)

    w_out_b = w_out.astype(MXU_DT)
    nw_t = jnp.tile(hgrn_norm_w, B_HEADS)
    x2 = _outproj(o_a, o_f, o_b, p, 7, nw_t, w_out_b, x2, g1, T, B_DV, False)
    if need_ctx:
        oc_a = _diff_attention(lam, pc, pc, None, subln_w, L, L, 0, 1.0 - lambda_init)
        xc2 = _outproj(oc_a, oc_f, oc_b, pc, 7, nw_t, w_out_b, xc2, g1c, L, B_DV, False)
    return x2, xc2


def _odd_layer(x2, xc2, mods, modc, norm_w, w_in, w_out, conv_w, a_log, dt_bias, gdn_norm_w, sink, rope_tabs, T, L):
    B = x2.shape[0] // T
    sh1, sc1, g1 = mods
    sh1c, sc1c, _ = modc
    c_qkv, c_z, c_g, c_q, c_kv = np.cumsum((0, C_HEADS * (2 * C_DK + C_DV), C_HEADS * C_DV, 4 * C_HEADS,
                                            D_HEADS * D_DH))[:5]
    end = c_kv + 2 * D_KV_HEADS * D_DH
    w_r = jnp.concatenate([w_in[:, c_qkv:c_g], w_in[:, c_q:end], w_in[:, c_g:c_q],
                           jnp.zeros((D_MODEL, ODD_IN_PAD - end), w_in.dtype)], axis=1).astype(MXU_DT)
    p = _inproj(x2, sh1, sc1, norm_w, w_r, T, rope=ODD_ROPE, rope_tabs=rope_tabs)
    pc = _inproj(xc2, sh1c, sc1c, norm_w, w_r, L)

    qkv = _gdn_prep(p, conv_w, T)
    qkvc = _gdn_prep(pc, conv_w, L)
    pad = lambda v: jnp.pad(v.reshape(1, -1), ((0, 0), (0, LANES - 2 * C_HEADS)))
    nea = pad(-jnp.exp(a_log))
    dtv = pad(dt_bias)
    s0 = jnp.zeros((B, C_HEADS, C_DK, C_DV), jnp.float32)
    _, _, sf, sb = _gdn_scan(qkvc, pc, nea, dtv, s0, s0, L)
    o_f, o_b, _, _ = _gdn_scan(qkv, p, nea, dtv, sf, sb, T)

    o_d = _window_attention(sink, p, pc, T, L)
    x2 = _outproj(o_d, o_f, o_b, p, 3, jnp.tile(gdn_norm_w, C_HEADS), w_out.astype(MXU_DT), x2, g1, T, C_DV, True)
    return x2


def kernel(x, c, ctx, c_ctx, ada_w, ada_b, norm_w, final_norm_w, even_w_in, even_w_out, diff_lambda, diff_subln_w, hgrn_lb, hgrn_norm_w, odd_w_in, odd_w_out, gdn_conv_w, gdn_a_log, gdn_dt_bias, gdn_norm_w, swa_sink, peer_w_q, peer_keys, peer_u, peer_v):
    B, T, D = x.shape
    L = ctx.shape[1]
    rope_tabs = _rope_tables(T)
    x2 = x.reshape(B * T, D)
    xc2 = ctx.reshape(B * L, D)
    cvecs = jnp.concatenate([c, jnp.broadcast_to(c_ctx[None, :], (SUBLANES, D))], axis=0)
    for layer in range(DEPTH):
        need_ctx = layer < DEPTH - 1
        ada = _ada_linear(cvecs, ada_w[layer], ada_b[layer])
        lat = [ada[:B, j * D:(j + 1) * D].reshape(B, 1, D) for j in range(ADA_CHUNKS)]
        cx = [jnp.broadcast_to(ada[B:B + 1, j * D:(j + 1) * D].reshape(1, 1, D), (B, 1, D)) for j in range(ADA_CHUNKS)]
        if layer % 2 == 0:
            e = layer // 2
            x2, xc2 = _even_layer(layer, x2, xc2, lat[:3], cx[:3], norm_w[layer, 0], even_w_in[e], even_w_out[e],
                                  diff_lambda[e], diff_subln_w[e], hgrn_lb, hgrn_norm_w[e], rope_tabs, T, L, need_ctx)
        else:
            o = layer // 2
            x2 = _odd_layer(x2, xc2, lat[:3], cx[:3], norm_w[layer, 0], odd_w_in[o], odd_w_out[o], gdn_conv_w[o],
                            gdn_a_log[o], gdn_dt_bias[o], gdn_norm_w[o], swa_sink[o], rope_tabs, T, L)
        wq_t = peer_w_q[layer].T
        tab = jnp.concatenate([peer_u[layer], peer_v[layer]], axis=-1).reshape(-1, ROW_TILES, LANES)
        x2 = _peer_block(x2, norm_w[layer, 1], lat[3], lat[4], lat[5], wq_t, peer_keys[layer], tab, T)
        if need_ctx:
            xc2 = _peer_block(xc2, norm_w[layer, 1], cx[3], cx[4], cx[5], wq_t, peer_keys[layer], tab, L)
    return _final_norm(x2, final_norm_w).reshape(B, T, D)
```

```python
import functools
import math

import jax
import jax.numpy as jnp
import numpy as np
from jax import lax
from jax.experimental import pallas as pl
from jax.experimental.pallas import tpu as pltpu

D_MODEL = 1024
DEPTH = 2
GRID_W = 64
ADA_CHUNKS = 6
NORM_EPS = 1e-6
ROPE_THETA = 10000.0
ROPE_DIM = 64
ROPE_FREQS = ROPE_DIM // 4

A_HEADS = 4
A_DH = 64
A_DV = 2 * A_DH
LAMBDA_INIT_BASE = 0.8
LAMBDA_INIT_SPAN = 0.6
LAMBDA_INIT_RATE = 0.3
B_HEADS = 8
B_DK = 64
B_DV = 64
B_CHUNK = 32
C_HEADS = 4
C_DK = 128
C_DV = 128
C_CHUNK = 64
C_CONV = 5
D_HEADS = 8
D_KV_HEADS = 2
D_DH = 64
D_WINDOW = 128
P_HEADS = 8
P_NKEYS = 128
P_DQ = 256
P_TOPK = 16

SUBLANES = 8
LANES = 128
VMEM_LIMIT = 48 * 1024 * 1024

MXU_DT = jnp.bfloat16
HI = lax.Precision.HIGHEST

HALF = 512
EVEN_IN = 4096
ODD_IN_PAD = 2944
ODD_ROPE = (2048, 2688)
EVEN_ROPE = (0, 1024)

PROJ_TM = 256
ATT_TQ = 256
SWA_TQ = 128
PREP_TR = 256

P_NSEL = P_HEADS * P_TOPK
ROUTE_TN = 256
APPLY_TB = 8
ROW_TILES = 2 * D_MODEL // LANES
ROW_GROUPS = P_NSEL // SUBLANES


def _cparams(n_axes=1):
    return pltpu.CompilerParams(dimension_semantics=("arbitrary",) * n_axes, vmem_limit_bytes=VMEM_LIMIT)


def _mm(a, b):
    return jnp.dot(a.astype(MXU_DT), b.astype(MXU_DT), preferred_element_type=jnp.float32)


def _mm_nt(a, b):
    return lax.dot_general(a.astype(MXU_DT), b.astype(MXU_DT), (((1,), (1,)), ((), ())),
                           preferred_element_type=jnp.float32)


def _mm_tn(a, b):
    return lax.dot_general(a.astype(MXU_DT), b.astype(MXU_DT), (((0,), (0,)), ((), ())),
                           preferred_element_type=jnp.float32)


def _mm_hi(a, b):
    return jnp.dot(a, b, precision=HI, preferred_element_type=jnp.float32)


def _mm_x3(a, b):
    a_hi, b_hi = a.astype(MXU_DT), b.astype(MXU_DT)
    a_lo = (a - a_hi.astype(jnp.float32)).astype(MXU_DT)
    b_lo = (b - b_hi.astype(jnp.float32)).astype(MXU_DT)
    dot = functools.partial(jnp.dot, preferred_element_type=jnp.float32)
    return dot(a_hi, b_hi) + (dot(a_hi, b_lo) + dot(a_lo, b_hi))


def _silu(x):
    return x * jax.nn.sigmoid(x)


def _ada_kernel(c_ref, w_ref, b_ref, o_ref):
    o_ref[...] = _mm(_silu(c_ref[...]), w_ref[...]) + b_ref[...]


def _ada_linear(cvecs, w, b):
    rows = cvecs.shape[0]
    return pl.pallas_call(
        _ada_kernel,
        grid=(ADA_CHUNKS,),
        in_specs=[pl.BlockSpec((rows, D_MODEL), lambda j: (0, 0)),
                  pl.BlockSpec((D_MODEL, D_MODEL), lambda j: (0, j)),
                  pl.BlockSpec((1, D_MODEL), lambda j: (0, j))],
        out_specs=pl.BlockSpec((rows, D_MODEL), lambda j: (0, j)),
        out_shape=jax.ShapeDtypeStruct((rows, ADA_CHUNKS * D_MODEL), jnp.float32),
        compiler_params=_cparams(), name="ada_linear",
    )(cvecs, w, b.reshape(1, -1))


def _inproj_kernel(x_ref, sh_ref, sc_ref, nw_ref, w_ref, *rest, rope):
    o_ref = rest[-1]
    x = x_ref[...]
    h = x * lax.rsqrt(jnp.mean(x * x, axis=-1, keepdims=True) + NORM_EPS) * nw_ref[...]
    h = h * (1.0 + sc_ref[0]) + sh_ref[0]
    y = jnp.dot(h.astype(MXU_DT), w_ref[...], preferred_element_type=jnp.float32)
    o_ref[...] = y
    if rope is not None:
        cos, sin_up, sin_dn = rest[0][...], rest[1][...], rest[2][...]
        for c in range(rope[0] // LANES, rope[1] // LANES):
            t = y[:, c * LANES:(c + 1) * LANES]
            o_ref[:, c * LANES:(c + 1) * LANES] = (t * cos + pltpu.roll(t, LANES - 16, 1) * sin_up
                                                   + pltpu.roll(t, 16, 1) * sin_dn)


def _inproj(x2, shift, scale, norm_w, w, tokens_per_batch, rope=None, rope_tabs=None):
    n_tok, f_out = x2.shape[0], w.shape[1]
    tpb = tokens_per_batch // PROJ_TM
    in_specs = [pl.BlockSpec((PROJ_TM, D_MODEL), lambda i: (i, 0)),
                pl.BlockSpec((1, 1, D_MODEL), lambda i: (i // tpb, 0, 0)),
                pl.BlockSpec((1, 1, D_MODEL), lambda i: (i // tpb, 0, 0)),
                pl.BlockSpec((1, D_MODEL), lambda i: (0, 0)),
                pl.BlockSpec((D_MODEL, f_out), lambda i: (0, 0))]
    args = [x2, shift, scale, norm_w.reshape(1, D_MODEL), w]
    if rope is not None:
        in_specs += [pl.BlockSpec((PROJ_TM, LANES), lambda i: (i % tpb, 0))] * 3
        args += list(rope_tabs)
    return pl.pallas_call(
        functools.partial(_inproj_kernel, rope=rope),
        grid=(n_tok // PROJ_TM,),
        in_specs=in_specs,
        out_specs=pl.BlockSpec((PROJ_TM, f_out), lambda i: (i, 0)),
        out_shape=jax.ShapeDtypeStruct((n_tok, f_out), jnp.float32),
        compiler_params=_cparams(), name="inproj",
    )(*args)


def _rope_tables(seq):
    rows = seq // GRID_W
    row = jnp.repeat(jnp.arange(rows, dtype=jnp.float32), GRID_W)
    col = jnp.tile(jnp.arange(GRID_W, dtype=jnp.float32), rows)
    inv = ROPE_THETA ** (-jnp.arange(ROPE_FREQS, dtype=jnp.float32) / ROPE_FREQS)
    ar = row[:, None] * inv[None, :]
    ac = col[:, None] * inv[None, :]
    ang = jnp.concatenate([ar, ar, ac, ac], axis=-1)
    cos = jnp.tile(jnp.cos(ang), (1, 2))
    sin = jnp.tile(jnp.sin(ang), (1, 2))
    quarter = (jnp.arange(LANES) % ROPE_DIM) // ROPE_FREQS
    sin_up = jnp.where(quarter % 2 == 0, -sin, 0.0)
    sin_dn = jnp.where(quarter % 2 == 1, sin, 0.0)
    return cos, sin_up, sin_dn


def _diffattn_kernel(lam_ref, q_ref, kc_ref, vc_ref, *rest, has_lat, out_scale):
    if has_lat:
        kl_ref, vl_ref, sw_ref, o_ref = rest
    else:
        sw_ref, o_ref = rest
    lam = lam_ref[0]
    q = q_ref[...] * (A_DH ** -0.5)
    lane = lax.broadcasted_iota(jnp.int32, (1, LANES), 1)
    first = lane < A_DH
    kc = kc_ref[...]
    probs = []
    for m in range(2):
        qm = jnp.where(first if m == 0 else jnp.logical_not(first), q, 0.0)
        s_c = _mm_nt(qm, kc)
        mx = jnp.max(s_c, axis=-1, keepdims=True)
        if has_lat:
            s_l = _mm_nt(qm, kl_ref[...])
            mx = jnp.maximum(mx, jnp.max(s_l, axis=-1, keepdims=True))
            e_l = jnp.exp(s_l - mx)
        e_c = jnp.exp(s_c - mx)
        den = jnp.sum(e_c, axis=-1, keepdims=True)
        if has_lat:
            den = den + jnp.sum(e_l, axis=-1, keepdims=True)
        probs.append((e_c / den, (e_l / den) if has_lat else None))
    a_c = probs[0][0] - lam * probs[1][0]
    o = _mm(a_c, vc_ref[...])
    if has_lat:
        a_l = probs[0][1] - lam * probs[1][1]
        o = o + _mm(a_l, vl_ref[...])
    o = o * lax.rsqrt(jnp.mean(o * o, axis=-1, keepdims=True) + NORM_EPS) * sw_ref[...]
    o_ref[...] = o * out_scale


def _diff_attention(lam, pq, pc, pl_lat, subln_w, n_q, n_ctx, n_lat, out_scale):
    batch = pq.shape[0] // n_q
    tq = min(ATT_TQ, n_q)
    nq = n_q // tq
    has_lat = pl_lat is not None
    in_specs = [pl.BlockSpec(memory_space=pltpu.SMEM),
                pl.BlockSpec((tq, LANES), lambda b, h, i: (b * nq + i, h)),
                pl.BlockSpec((n_ctx, LANES), lambda b, h, i: (b, A_HEADS + h)),
                pl.BlockSpec((n_ctx, LANES), lambda b, h, i: (b, 2 * A_HEADS + h))]
    args = [lam, pq, pc, pc]
    if has_lat:
        in_specs += [pl.BlockSpec((n_lat, LANES), lambda b, h, i: (b, A_HEADS + h)),
                     pl.BlockSpec((n_lat, LANES), lambda b, h, i: (b, 2 * A_HEADS + h))]
        args += [pl_lat, pl_lat]
    in_specs.append(pl.BlockSpec((1, LANES), lambda b, h, i: (0, 0)))
    args.append(subln_w.reshape(1, A_DV))
    return pl.pallas_call(
        functools.partial(_diffattn_kernel, has_lat=has_lat, out_scale=out_scale),
        grid=(batch, A_HEADS, nq),
        in_specs=in_specs,
        out_specs=pl.BlockSpec((tq, LANES), lambda b, h, i: (b * nq + i, h)),
        out_shape=jax.ShapeDtypeStruct((pq.shape[0], HALF), jnp.float32),
        compiler_params=_cparams(3), name="diff_attn",
    )(*args)


def _tri(n, upper):
    r = lax.broadcasted_iota(jnp.int32, (n, n), 0)
    c = lax.broadcasted_iota(jnp.int32, (n, n), 1)
    return (c >= r) if upper else (c <= r)


def _hgrn_dir(q_ref, i_ref, f_ref, lb, s_ref, o_ref, rev):
    n = B_CHUNK
    incl = _tri(n, rev)
    f = lb + (1.0 - lb) * jax.nn.sigmoid(f_ref[...])
    k = 1.0 - f
    g = jnp.log(f)
    qh = _silu(q_ref[...]) * (B_DK ** -0.5)
    v = i_ref[...]
    b = _mm_hi(incl.astype(jnp.float32), g)
    mid = n // 2
    ref_row = (n - 1 - mid) if rev else mid
    last_row = 0 if rev else n - 1
    ref = b[ref_row:ref_row + 1, :]
    b_last = b[last_row:last_row + 1, :]
    q_in = qh * jnp.exp(b - ref)
    k_in = k * jnp.exp(ref - b)
    q_st = qh * jnp.exp(b)
    k_st = k * jnp.exp(b_last - b)
    decay = jnp.exp(b_last)
    outs = []
    for h in range(B_HEADS):
        sl = slice(h * B_DK, (h + 1) * B_DK)
        st = s_ref[0, h]
        att = jnp.where(incl, _mm_nt(q_in[:, sl], k_in[:, sl]), 0.0)
        outs.append(_mm(att, v[:, sl]) + _mm_nt(q_st[:, sl], st))
        s_ref[0, h] = st * decay[:, sl] + _mm_tn(v[:, sl], k_st[:, sl])
    o_ref[...] = jnp.concatenate(outs, axis=-1)


def _hgrn_kernel(qf, if_, ff, qb, ib, fb, lb_ref, s0f, s0b, of, ob, sf, sb):
    @pl.when(pl.program_id(1) == 0)
    def _():
        sf[...] = s0f[...]
        sb[...] = s0b[...]
    lb = lb_ref[...]
    _hgrn_dir(qf, if_, ff, lb, sf, of, False)
    _hgrn_dir(qb, ib, fb, lb, sb, ob, True)


def _hgrn_scan(p, lb, s0f, s0b, n):
    batch = p.shape[0] // n
    nc = n // B_CHUNK
    blk = lambda col, rev: pl.BlockSpec(
        (B_CHUNK, HALF), (lambda b, c: (b * nc + nc - 1 - c, col)) if rev else (lambda b, c: (b * nc + c, col)))
    st = pl.BlockSpec((1, B_HEADS, B_DK, B_DV), lambda b, c: (b, 0, 0, 0))
    o_sd = jax.ShapeDtypeStruct((p.shape[0], HALF), jnp.float32)
    s_sd = jax.ShapeDtypeStruct((batch, B_HEADS, B_DK, B_DV), jnp.float32)
    return pl.pallas_call(
        _hgrn_kernel,
        grid=(batch, nc),
        in_specs=[blk(3, False), blk(4, False), blk(5, False), blk(3, True), blk(4, True), blk(6, True),
                  pl.BlockSpec((1, HALF), lambda b, c: (0, 0)), st, st],
        out_specs=[blk(0, False), blk(0, True), st, st],
        out_shape=[o_sd, o_sd, s_sd, s_sd],
        compiler_params=_cparams(2), name="hgrn_scan",
    )(p, p, p, p, p, p, lb.reshape(1, HALF), s0f, s0b)


def _gdn_prep_kernel(x_ref, prev_ref, next_ref, w_ref, o_ref, *, tiles_per_seq):
    i = pl.program_id(0)
    first = (i % tiles_per_seq) == 0
    last = (i % tiles_per_seq) == tiles_per_seq - 1
    x = x_ref[...]
    prev = jnp.where(first, 0.0, prev_ref[...])
    nxt = jnp.where(last, 0.0, next_ref[...])
    xe = jnp.concatenate([prev, x, nxt], axis=0)
    pad = C_CONV // 2
    acc = jnp.zeros_like(x)
    for j in range(C_CONV):
        acc = acc + xe[SUBLANES + j - pad:SUBLANES + j - pad + PREP_TR, :] * w_ref[j:j + 1, :]
    y = _silu(acc)
    for c in range(3 * C_HEADS):
        t = y[:, c * LANES:(c + 1) * LANES]
        if c < 2 * C_HEADS:
            t = t * lax.rsqrt(jnp.sum(t * t, axis=-1, keepdims=True) + NORM_EPS)
            if c < C_HEADS:
                t = t * (C_DK ** -0.5)
        o_ref[:, c * LANES:(c + 1) * LANES] = t


def _gdn_prep(p, conv_w, n):
    width = 3 * HALF
    nt = p.shape[0] // PREP_TR
    tps = n // PREP_TR
    rb = PREP_TR // SUBLANES
    last_blk = p.shape[0] // SUBLANES - 1
    return pl.pallas_call(
        functools.partial(_gdn_prep_kernel, tiles_per_seq=tps),
        grid=(nt,),
        in_specs=[pl.BlockSpec((PREP_TR, width), lambda i: (i, 0)),
                  pl.BlockSpec((SUBLANES, width), lambda i: (jnp.maximum(i * rb - 1, 0), 0)),
                  pl.BlockSpec((SUBLANES, width), lambda i: (jnp.minimum((i + 1) * rb, last_blk), 0)),
                  pl.BlockSpec((SUBLANES, width), lambda i: (0, 0))],
        out_specs=pl.BlockSpec((PREP_TR, width), lambda i: (i, 0)),
        out_shape=jax.ShapeDtypeStruct((p.shape[0], width), jnp.float32),
        compiler_params=_cparams(), name="gdn_prep",
    )(p, p, p, jnp.pad(conv_w, ((0, SUBLANES - C_CONV), (0, 0))))


def _gdn_dir(q_ref, k_ref, v_ref, gt_ref, nea, dtv, s_ref, o_ref, d, rev):
    n = C_CHUNK
    gates = gt_ref[...]
    x = gates + dtv
    la = nea * (jnp.maximum(x, 0.0) + jnp.log1p(jnp.exp(-jnp.abs(x))))
    beta_all = jax.nn.sigmoid(gates)
    inclf = _tri(n, rev).astype(jnp.float32)
    g_cols = _mm_hi(inclf, la)
    g_rows = lax.dot_general(la, inclf, (((0,), (1,)), ((), ())), precision=HI,
                             preferred_element_type=jnp.float32)
    last_row = 0 if rev else n - 1
    heads = range(C_HEADS)
    cols = [d * C_HEADS + h for h in heads]
    stack = lambda ref: jnp.concatenate([ref[:, h * LANES:(h + 1) * LANES] for h in heads], axis=0)
    q, k, v = stack(q_ref), stack(k_ref), stack(v_ref)
    gc = jnp.concatenate([g_cols[:, c:c + 1] for c in cols], axis=0)
    gr = jnp.concatenate([g_rows[c:c + 1, :] for c in cols], axis=1)
    beta = jnp.concatenate([beta_all[:, 2 * C_HEADS + c:2 * C_HEADS + c + 1] for c in cols], axis=0)
    rows = C_HEADS * n
    r = lax.broadcasted_iota(jnp.int32, (rows, rows), 0)
    c = lax.broadcasted_iota(jnp.int32, (rows, rows), 1)
    same = (r // n) == (c // n)
    incl = jnp.logical_and(same, (c >= r) if rev else (c <= r))
    strict = jnp.logical_and(same, (c > r) if rev else (c < r))
    eye = (r == c).astype(jnp.float32)
    decay = jnp.where(incl, jnp.exp(jnp.where(incl, gc - gr, 0.0)), 0.0)
    kb = k * beta
    nmat = jnp.where(strict, _mm_nt(kb, k) * decay, 0.0)
    inv = eye - nmat
    pw = _mm_x3(nmat, nmat)
    for step in range(5):
        inv = inv + _mm_x3(inv, pw)
        if step < 4:
            pw = _mm_x3(pw, pw)
    e_gc = jnp.exp(gc)
    rhs = jnp.concatenate([v * beta, kb * e_gc], axis=-1)
    sol = _mm_x3(inv, rhs)
    u, w = sol[:, :C_DV], sol[:, C_DV:]
    hs = lambda a, h: a[h * n:(h + 1) * n, :]
    states = [s_ref[0, h] for h in heads]
    v_new = jnp.concatenate([hs(u, h) - _mm(hs(w, h), states[h]) for h in heads], axis=0)
    att = _mm_nt(q, k) * decay
    o_in = _mm(att, v_new)
    q_st = q * e_gc
    o_ref[...] = jnp.concatenate([hs(o_in, h) + _mm(hs(q_st, h), states[h]) for h in heads], axis=-1)
    for h in heads:
        g_last = gc[h * n + last_row:h * n + last_row + 1, :]
        k_st = hs(k, h) * jnp.exp(g_last - hs(gc, h))
        s_ref[0, h] = states[h] * jnp.exp(g_last) + _mm_tn(k_st, hs(v_new, h))


def _gdn_kernel(qf, kf, vf, gf, qb, kb, vb, gb, nea_ref, dt_ref, s0f, s0b, of, ob, sf, sb):
    @pl.when(pl.program_id(1) == 0)
    def _():
        sf[...] = s0f[...]
        sb[...] = s0b[...]
    nea, dtv = nea_ref[...], dt_ref[...]
    _gdn_dir(qf, kf, vf, gf, nea, dtv, sf, of, 0, False)
    _gdn_dir(qb, kb, vb, gb, nea, dtv, sb, ob, 1, True)


def _gdn_scan(qkv, p, nea, dtv, s0f, s0b, n):
    batch = p.shape[0] // n
    nc = n // C_CHUNK
    gate_col = (ODD_IN_PAD - LANES) // LANES

    def blk(width, col, rev):
        return pl.BlockSpec((C_CHUNK, width),
                            (lambda b, c: (b * nc + nc - 1 - c, col)) if rev else (lambda b, c: (b * nc + c, col)))
    st = pl.BlockSpec((1, C_HEADS, C_DK, C_DV), lambda b, c: (b, 0, 0, 0))
    vec = pl.BlockSpec((1, LANES), lambda b, c: (0, 0))
    o_sd = jax.ShapeDtypeStruct((p.shape[0], HALF), jnp.float32)
    s_sd = jax.ShapeDtypeStruct((batch, C_HEADS, C_DK, C_DV), jnp.float32)
    return pl.pallas_call(
        _gdn_kernel,
        grid=(batch, nc),
        in_specs=[blk(HALF, 0, False), blk(HALF, 1, False), blk(HALF, 2, False), blk(LANES, gate_col, False),
                  blk(HALF, 0, True), blk(HALF, 1, True), blk(HALF, 2, True), blk(LANES, gate_col, True),
                  vec, vec, st, st],
        out_specs=[blk(HALF, 0, False), blk(HALF, 0, True), st, st],
        out_shape=[o_sd, o_sd, s_sd, s_sd],
        compiler_params=_cparams(2), name="gdn_scan",
    )(qkv, qkv, qkv, p, qkv, qkv, qkv, p, nea, dtv, s0f, s0b)


def _swa_kernel(sink_ref, q_ref, kp_ref, k0_ref, kn_ref, vp_ref, v0_ref, vn_ref, kc_ref, vc_ref, o_ref, *, nblk):
    i = pl.program_id(1)
    tq = SWA_TQ
    q = q_ref[...] * (D_DH ** -0.5)
    kband = jnp.concatenate([kp_ref[...], k0_ref[...], kn_ref[...]], axis=0)
    vband = jnp.concatenate([vp_ref[...], v0_ref[...], vn_ref[...]], axis=0)
    qpos = lax.broadcasted_iota(jnp.int32, (tq, 3 * tq), 0)
    kpos = lax.broadcasted_iota(jnp.int32, (tq, 3 * tq), 1) - tq
    ok = jnp.abs(kpos - qpos) <= D_WINDOW
    ok = jnp.logical_and(ok, jnp.logical_or(kpos >= 0, i > 0))
    ok = jnp.logical_and(ok, jnp.logical_or(kpos < tq, i < nblk - 1))
    kc, vc = kc_ref[...], vc_ref[...]
    group = D_HEADS // D_KV_HEADS
    outs = []
    for h in range(D_HEADS):
        ks = slice((h // group) * D_DH, (h // group + 1) * D_DH)
        qh = q[:, h * D_DH:(h + 1) * D_DH]
        s_b = jnp.where(ok, _mm_nt(qh, kband[:, ks]), -jnp.inf)
        s_c = _mm_nt(qh, kc[:, ks])
        sink = sink_ref[h]
        mx = jnp.maximum(jnp.maximum(jnp.max(s_b, axis=-1, keepdims=True), jnp.max(s_c, axis=-1, keepdims=True)), sink)
        e_b = jnp.exp(s_b - mx)
        e_c = jnp.exp(s_c - mx)
        den = jnp.exp(sink - mx) + jnp.sum(e_b, axis=-1, keepdims=True) + jnp.sum(e_c, axis=-1, keepdims=True)
        outs.append(_mm(e_c / den, vc[:, ks]) + _mm(e_b / den, vband[:, ks]))
    o_ref[...] = jnp.concatenate(outs, axis=-1)


def _window_attention(sink, p, pc, n_lat, n_ctx):
    batch = p.shape[0] // n_lat
    nblk = n_lat // SWA_TQ
    qcol = ODD_ROPE[0] // HALF
    kcol = (ODD_ROPE[0] + HALF) // LANES
    vcol = kcol + 1
    prev = lambda col: pl.BlockSpec((SWA_TQ, LANES), lambda b, i: (b * nblk + jnp.maximum(i - 1, 0), col))
    cur = lambda col: pl.BlockSpec((SWA_TQ, LANES), lambda b, i: (b * nblk + i, col))
    nxt = lambda col: pl.BlockSpec((SWA_TQ, LANES), lambda b, i: (b * nblk + jnp.minimum(i + 1, nblk - 1), col))
    return pl.pallas_call(
        functools.partial(_swa_kernel, nblk=nblk),
        grid=(batch, nblk),
        in_specs=[pl.BlockSpec(memory_space=pltpu.SMEM),
                  pl.BlockSpec((SWA_TQ, HALF), lambda b, i: (b * nblk + i, qcol)),
                  prev(kcol), cur(kcol), nxt(kcol), prev(vcol), cur(vcol), nxt(vcol),
                  pl.BlockSpec((n_ctx, LANES), lambda b, i: (b, kcol)),
                  pl.BlockSpec((n_ctx, LANES), lambda b, i: (b, vcol))],
        out_specs=pl.BlockSpec((SWA_TQ, HALF), lambda b, i: (b * nblk + i, 0)),
        out_shape=jax.ShapeDtypeStruct((p.shape[0], HALF), jnp.float32),
        compiler_params=_cparams(2), name="window_attn",
    )(sink, p, p, p, p, p, p, p, pc, pc)


def _outproj_kernel(att_ref, of_ref, ob_ref, g_ref, nw_ref, w_ref, x_ref, gate_ref, o_ref, *, group, rec_first):
    o = of_ref[...] + ob_ref[...]
    sq = o * o
    if group == LANES:
        parts = [jnp.broadcast_to(jnp.mean(sq[:, c * LANES:(c + 1) * LANES], axis=-1, keepdims=True),
                                  (sq.shape[0], LANES)) for c in range(HALF // LANES)]
        ms = jnp.concatenate(parts, axis=-1)
    else:
        r = lax.broadcasted_iota(jnp.int32, (HALF, HALF), 0) // group
        c = lax.broadcasted_iota(jnp.int32, (HALF, HALF), 1) // group
        ms = _mm_hi(sq, (r == c).astype(jnp.float32)) * (1.0 / group)
    rec = o * lax.rsqrt(ms + NORM_EPS) * nw_ref[...] * _silu(g_ref[...])
    att = att_ref[...]
    top, bot = (rec, att) if rec_first else (att, rec)
    y = _mm(top, w_ref[:HALF, :]) + _mm(bot, w_ref[HALF:, :])
    o_ref[...] = x_ref[...] + gate_ref[0] * y


def _outproj(att, of, ob, p, gate_col, norm_w, w_out, x2, gate, tokens_per_batch, group, rec_first):
    n_tok = x2.shape[0]
    tpb = tokens_per_batch // PROJ_TM
    row = lambda col: pl.BlockSpec((PROJ_TM, HALF), lambda i: (i, col))
    return pl.pallas_call(
        functools.partial(_outproj_kernel, group=group, rec_first=rec_first),
        grid=(n_tok // PROJ_TM,),
        in_specs=[row(0), row(0), row(0), row(gate_col),
                  pl.BlockSpec((1, HALF), lambda i: (0, 0)),
                  pl.BlockSpec((2 * HALF, D_MODEL), lambda i: (0, 0)),
                  pl.BlockSpec((PROJ_TM, D_MODEL), lambda i: (i, 0)),
                  pl.BlockSpec((1, 1, D_MODEL), lambda i: (i // tpb, 0, 0))],
        out_specs=pl.BlockSpec((PROJ_TM, D_MODEL), lambda i: (i, 0)),
        out_shape=jax.ShapeDtypeStruct((n_tok, D_MODEL), jnp.float32),
        compiler_params=_cparams(), name="outproj",
    )(att, of, ob, p, norm_w.reshape(1, HALF), w_out, x2, gate)


def _top16_rows(s, payload=None):
    rows = s.shape[0]
    iota = lax.broadcasted_iota(jnp.int32, s.shape, 0)
    vals, poss, pays = [], [], []
    for _ in range(P_TOPK):
        m = jnp.max(s, axis=0, keepdims=True)
        pos = jnp.min(jnp.where(s == m, iota, rows), axis=0, keepdims=True)
        hit = iota == pos
        if payload is not None:
            pays.append(jnp.max(jnp.where(hit, payload, -1), axis=0, keepdims=True))
        vals.append(m)
        poss.append(pos)
        s = jnp.where(hit, -jnp.inf, s)
    cat = lambda xs: jnp.concatenate(xs, axis=0)
    return cat(vals), cat(poss), (cat(pays) if payload is not None else None)


def _route_kernel(x_ref, sh_ref, sc_ref, nw_ref, wqt_ref, keys_ref, h_ref, idx_ref, gate_ref):
    x = x_ref[...]
    h = x * lax.rsqrt(jnp.mean(x * x, axis=-1, keepdims=True) + NORM_EPS) * nw_ref[...]
    h = h * (1.0 + sc_ref[0]) + sh_ref[0]
    h_ref[...] = h
    qt = lax.dot_general(wqt_ref[...], h, (((1,), (1,)), ((), ())), precision=HI,
                         preferred_element_type=jnp.float32)
    idx_rows, gate_rows = [], []
    for hh in range(P_HEADS):
        tops = []
        for p in range(2):
            j = hh * 2 + p
            s = _mm_hi(keys_ref[j], qt[j * P_NKEYS:(j + 1) * P_NKEYS, :])
            v, i, _ = _top16_rows(s)
            tops.append((v, i))
        (v0, i0), (v1, i1) = tops
        nb = [P_TOPK // (a + 1) for a in range(P_TOPK)]
        n_pad = -sum(nb) % SUBLANES
        cand = jnp.concatenate([v0[a:a + 1, :] + v1[:nb[a], :] for a in range(P_TOPK)]
                               + [jnp.full((n_pad, v0.shape[1]), -jnp.inf, jnp.float32)], axis=0)
        cidx = jnp.concatenate([i0[a:a + 1, :] * P_NKEYS + i1[:nb[a], :] for a in range(P_TOPK)]
                               + [jnp.zeros((n_pad, v0.shape[1]), jnp.int32)], axis=0)
        best, _, sel = _top16_rows(cand, cidx)
        e = jnp.exp(best - best[0:1, :])
        gate_rows.append(e / jnp.sum(e, axis=0, keepdims=True))
        idx_rows.append(sel)
    idx_t = jnp.concatenate(idx_rows, axis=0)
    gate_t = jnp.concatenate(gate_rows, axis=0)
    idx_ref[...] = idx_t.T
    for j in range(ROUTE_TN // APPLY_TB):
        gate_ref[j] = gate_t[:, j * APPLY_TB:(j + 1) * APPLY_TB]


def _peer_route(x2, shift, scale, norm_w, wq_t, keys, tokens_per_batch):
    n_tok = x2.shape[0]
    tiles_per_batch = tokens_per_batch // ROUTE_TN
    nkeys2 = 2 * P_HEADS
    return pl.pallas_call(
        _route_kernel,
        grid=(n_tok // ROUTE_TN,),
        in_specs=[
            pl.BlockSpec((ROUTE_TN, D_MODEL), lambda i: (i, 0)),
            pl.BlockSpec((1, 1, D_MODEL), lambda i: (i // tiles_per_batch, 0, 0)),
            pl.BlockSpec((1, 1, D_MODEL), lambda i: (i // tiles_per_batch, 0, 0)),
            pl.BlockSpec((1, D_MODEL), lambda i: (0, 0)),
            pl.BlockSpec((P_HEADS * P_DQ, D_MODEL), lambda i: (0, 0)),
            pl.BlockSpec((nkeys2, P_NKEYS, P_DQ // 2), lambda i: (0, 0, 0)),
        ],
        out_specs=[
            pl.BlockSpec((ROUTE_TN, D_MODEL), lambda i: (i, 0)),
            pl.BlockSpec((ROUTE_TN, P_NSEL), lambda i: (i, 0)),
            pl.BlockSpec((ROUTE_TN // APPLY_TB, P_NSEL, APPLY_TB), lambda i: (i, 0, 0)),
        ],
        out_shape=[
            jax.ShapeDtypeStruct((n_tok, D_MODEL), jnp.float32),
            jax.ShapeDtypeStruct((n_tok, P_NSEL), jnp.int32),
            jax.ShapeDtypeStruct((n_tok // APPLY_TB, P_NSEL, APPLY_TB), jnp.float32),
        ],
        compiler_params=_cparams(), name="peer_route",
    )(x2, shift, scale, norm_w.reshape(1, D_MODEL), wq_t, keys.reshape(nkeys2, P_NKEYS, P_DQ // 2))


def _apply_kernel(idx_cur, idx_nxt, h_ref, gate_ref, x_ref, g2_ref, tab_ref, o_ref, buf, sem):
    i = pl.program_id(0)
    n = pl.num_programs(0)
    slot = i % 2

    def row_copy(idx_ref, s, t, r):
        return pltpu.make_async_copy(
            tab_ref.at[idx_ref[t, r]],
            buf.at[s, t * ROW_GROUPS + r // SUBLANES, :, r % SUBLANES, :],
            sem.at[s])

    def slot_wait(s):
        pltpu.make_async_copy(buf.at[s], buf.at[s], sem.at[s]).wait()

    @pl.when(i == 0)
    def _():
        def body(t, carry):
            for r in range(P_NSEL):
                row_copy(idx_cur, 0, t, r).start(priority=r % 2)
            return carry
        lax.fori_loop(0, APPLY_TB, body, 0)

    slot_wait(slot)

    def issue_next(t, r0, r1):
        for r in range(r0, r1):
            row_copy(idx_nxt, 1 - slot, t, r).start(priority=r % 2)

    half = ROW_TILES // 2
    lane = lax.broadcasted_iota(jnp.int32, (P_NSEL, APPLY_TB), 1)
    a = jnp.zeros((P_NSEL, APPLY_TB), jnp.float32)
    for t in range(APPLY_TB):
        issue_next(t, 0, P_NSEL // 2)
        u = buf[slot, t * ROW_GROUPS:(t + 1) * ROW_GROUPS, :half]
        xt = h_ref[t:t + 1, :].reshape(1, half, 1, LANES)
        s = jnp.sum(u * xt, axis=1)
        at = jnp.sum(s, axis=-1, keepdims=True).reshape(P_NSEL, 1)
        a = jnp.where(lane == t, at, a)
    coef = gate_ref[0] * (0.5 * a * (1.0 + lax.erf(a * (2.0 ** -0.5))))
    for t in range(APPLY_TB):
        issue_next(t, P_NSEL // 2, P_NSEL)
        v = buf[slot, t * ROW_GROUPS:(t + 1) * ROW_GROUPS, half:]
        ct = coef[:, t:t + 1].reshape(ROW_GROUPS, 1, SUBLANES, 1)
        o = jnp.sum(jnp.sum(ct * v, axis=0), axis=1)
        o_ref[t:t + 1, :] = x_ref[t:t + 1, :] + g2_ref[0] * o.reshape(1, D_MODEL)

    @pl.when(i == n - 1)
    def _():
        slot_wait(1 - slot)


def _peer_apply(idx, h, gate_t, x2, g2, tab, tokens_per_batch):
    n_tok = h.shape[0]
    nt = n_tok // APPLY_TB
    tpb = tokens_per_batch // APPLY_TB
    return pl.pallas_call(
        _apply_kernel,
        grid=(nt,),
        in_specs=[
            pl.BlockSpec((APPLY_TB, P_NSEL), lambda i: (i, 0), memory_space=pltpu.SMEM),
            pl.BlockSpec((APPLY_TB, P_NSEL), lambda i: (jnp.minimum(i + 1, nt - 1), 0), memory_space=pltpu.SMEM),
            pl.BlockSpec((APPLY_TB, D_MODEL), lambda i: (i, 0)),
            pl.BlockSpec((1, P_NSEL, APPLY_TB), lambda i: (i, 0, 0)),
            pl.BlockSpec((APPLY_TB, D_MODEL), lambda i: (i, 0)),
            pl.BlockSpec((1, 1, D_MODEL), lambda i: (i // tpb, 0, 0)),
            pl.BlockSpec(memory_space=pl.ANY),
        ],
        out_specs=pl.BlockSpec((APPLY_TB, D_MODEL), lambda i: (i, 0)),
        out_shape=jax.ShapeDtypeStruct((n_tok, D_MODEL), jnp.float32),
        scratch_shapes=[pltpu.VMEM((2, APPLY_TB * ROW_GROUPS, ROW_TILES, SUBLANES, LANES), jnp.float32),
                        pltpu.SemaphoreType.DMA((2,))],
        compiler_params=_cparams(), name="peer_apply",
    )(idx, idx, h, gate_t, x2, g2, tab)


def _peer_block(x2, norm_w, shift, scale, g2, wq_t, keys, tab, tokens_per_batch):
    h, idx, gate_t = _peer_route(x2, shift, scale, norm_w, wq_t, keys, tokens_per_batch)
    return _peer_apply(idx, h, gate_t, x2, g2, tab, tokens_per_batch)


def _final_norm_kernel(x_ref, w_ref, o_ref):
    x = x_ref[...]
    o_ref[...] = x * lax.rsqrt(jnp.mean(x * x, axis=-1, keepdims=True) + NORM_EPS) * w_ref[...]


def _final_norm(x2, w):
    tm = 512
    return pl.pallas_call(
        _final_norm_kernel,
        grid=(x2.shape[0] // tm,),
        in_specs=[pl.BlockSpec((tm, D_MODEL), lambda i: (i, 0)), pl.BlockSpec((1, D_MODEL), lambda i: (0, 0))],
        out_specs=pl.BlockSpec((tm, D_MODEL), lambda i: (i, 0)),
        out_shape=jax.ShapeDtypeStruct(x2.shape, jnp.float32),
        compiler_params=_cparams(), name="final_norm",
    )(x2, w.reshape(1, D_MODEL))


def _even_layer(layer, x2, xc2, mods, modc, norm_w, w_in, w_out, diff_lambda, subln_w, hgrn_lb, hgrn_norm_w,
                rope_tabs, T, L, need_ctx):
    B = x2.shape[0] // T
    sh1, sc1, g1 = mods
    sh1c, sc1c, g1c = modc
    w_in_b = w_in.astype(MXU_DT)
    p = _inproj(x2, sh1, sc1, norm_w, w_in_b, T, rope=EVEN_ROPE, rope_tabs=rope_tabs)
    pc = _inproj(xc2, sh1c, sc1c, norm_w, w_in_b, L)

    lambda_init = LAMBDA_INIT_BASE - LAMBDA_INIT_SPAN * math.exp(-LAMBDA_INIT_RATE * layer)
    lam = (jnp.exp(jnp.sum(diff_lambda[0] * diff_lambda[1])) - jnp.exp(jnp.sum(diff_lambda[2] * diff_lambda[3]))
           + lambda_init).reshape(1)
    o_a = _diff_attention(lam, p, pc, p, subln_w, T, L, T, 1.0 - lambda_init)

    lb = jnp.cumsum(jax.nn.softmax(hgrn_lb, axis=0), axis=0)[layer]
    s0 = jnp.zeros((B, B_HEADS, B_DK, B_DV), jnp.float32)
    oc_f, oc_b, sf, sb = _hgrn_scan(pc, lb, s0, s0, L)
    o_f, o_b, _, _ = _hgrn_scan(p, lb, sf, sb, T)

    w_out_b = w_out.astype(MXU_DT)
    nw_t = jnp.tile(hgrn_norm_w, B_HEADS)
    x2 = _outproj(o_a, o_f, o_b, p, 7, nw_t, w_out_b, x2, g1, T, B_DV, False)
    if need_ctx:
        oc_a = _diff_attention(lam, pc, pc, None, subln_w, L, L, 0, 1.0 - lambda_init)
        xc2 = _outproj(oc_a, oc_f, oc_b, pc, 7, nw_t, w_out_b, xc2, g1c, L, B_DV, False)
    return x2, xc2


def _odd_layer(x2, xc2, mods, modc, norm_w, w_in, w_out, conv_w, a_log, dt_bias, gdn_norm_w, sink, rope_tabs, T, L):
    B = x2.shape[0] // T
    sh1, sc1, g1 = mods
    sh1c, sc1c, _ = modc
    c_qkv, c_z, c_g, c_q, c_kv = np.cumsum((0, C_HEADS * (2 * C_DK + C_DV), C_HEADS * C_DV, 4 * C_HEADS,
                                            D_HEADS * D_DH))[:5]
    end = c_kv + 2 * D_KV_HEADS * D_DH
    w_r = jnp.concatenate([w_in[:, c_qkv:c_g], w_in[:, c_q:end], w_in[:, c_g:c_q],
                           jnp.zeros((D_MODEL, ODD_IN_PAD - end), w_in.dtype)], axis=1).astype(MXU_DT)
    p = _inproj(x2, sh1, sc1, norm_w, w_r, T, rope=ODD_ROPE, rope_tabs=rope_tabs)
    pc = _inproj(xc2, sh1c, sc1c, norm_w, w_r, L)

    qkv = _gdn_prep(p, conv_w, T)
    qkvc = _gdn_prep(pc, conv_w, L)
    pad = lambda v: jnp.pad(v.reshape(1, -1), ((0, 0), (0, LANES - 2 * C_HEADS)))
    nea = pad(-jnp.exp(a_log))
    dtv = pad(dt_bias)
    s0 = jnp.zeros((B, C_HEADS, C_DK, C_DV), jnp.float32)
    _, _, sf, sb = _gdn_scan(qkvc, pc, nea, dtv, s0, s0, L)
    o_f, o_b, _, _ = _gdn_scan(qkv, p, nea, dtv, sf, sb, T)

    o_d = _window_attention(sink, p, pc, T, L)
    x2 = _outproj(o_d, o_f, o_b, p, 3, jnp.tile(gdn_norm_w, C_HEADS), w_out.astype(MXU_DT), x2, g1, T, C_DV, True)
    return x2


def kernel(x, c, ctx, c_ctx, ada_w, ada_b, norm_w, final_norm_w, even_w_in, even_w_out, diff_lambda, diff_subln_w, hgrn_lb, hgrn_norm_w, odd_w_in, odd_w_out, gdn_conv_w, gdn_a_log, gdn_dt_bias, gdn_norm_w, swa_sink, peer_w_q, peer_keys, peer_u, peer_v):
    B, T, D = x.shape
    L = ctx.shape[1]
    rope_tabs = _rope_tables(T)
    x2 = x.reshape(B * T, D)
    xc2 = ctx.reshape(B * L, D)
    cvecs = jnp.concatenate([c, jnp.broadcast_to(c_ctx[None, :], (SUBLANES, D))], axis=0)
    for layer in range(DEPTH):
        need_ctx = layer < DEPTH - 1
        ada = _ada_linear(cvecs, ada_w[layer], ada_b[layer])
        lat = [ada[:B, j * D:(j + 1) * D].reshape(B, 1, D) for j in range(ADA_CHUNKS)]
        cx = [jnp.broadcast_to(ada[B:B + 1, j * D:(j + 1) * D].reshape(1, 1, D), (B, 1, D)) for j in range(ADA_CHUNKS)]
        if layer % 2 == 0:
            e = layer // 2
            x2, xc2 = _even_layer(layer, x2, xc2, lat[:3], cx[:3], norm_w[layer, 0], even_w_in[e], even_w_out[e],
                                  diff_lambda[e], diff_subln_w[e], hgrn_lb, hgrn_norm_w[e], rope_tabs, T, L, need_ctx)
        else:
            o = layer // 2
            x2 = _odd_layer(x2, xc2, lat[:3], cx[:3], norm_w[layer, 0], odd_w_in[o], odd_w_out[o], gdn_conv_w[o],
                            gdn_a_log[o], gdn_dt_bias[o], gdn_norm_w[o], swa_sink[o], rope_tabs, T, L)
        wq_t = peer_w_q[layer].T
        tab = jnp.concatenate([peer_u[layer], peer_v[layer]], axis=-1).reshape(-1, ROW_TILES, LANES)
        x2 = _peer_block(x2, norm_w[layer, 1], lat[3], lat[4], lat[5], wq_t, peer_keys[layer], tab, T)
        if need_ctx:
            xc2 = _peer_block(xc2, norm_w[layer, 1], cx[3], cx[4], cx[5], wq_t, peer_keys[layer], tab, L)
    return _final_norm(x2, final_norm_w).reshape(B, T, D)
```

```python
import functools
import math

import jax
import jax.numpy as jnp
import numpy as np
from jax import lax
from jax.experimental import pallas as pl
from jax.experimental.pallas import tpu as pltpu

D_MODEL = 1024
DEPTH = 2
GRID_W = 64
ADA_CHUNKS = 6
NORM_EPS = 1e-6
ROPE_THETA = 10000.0
ROPE_DIM = 64
ROPE_FREQS = ROPE_DIM // 4

A_HEADS = 4
A_DH = 64
A_DV = 2 * A_DH
LAMBDA_INIT_BASE = 0.8
LAMBDA_INIT_SPAN = 0.6
LAMBDA_INIT_RATE = 0.3
B_HEADS = 8
B_DK = 64
B_DV = 64
B_CHUNK = 32
C_HEADS = 4
C_DK = 128
C_DV = 128
C_CHUNK = 64
C_CONV = 5
D_HEADS = 8
D_KV_HEADS = 2
D_DH = 64
D_WINDOW = 128
P_HEADS = 8
P_NKEYS = 128
P_DQ = 256
P_TOPK = 16

SUBLANES = 8
LANES = 128
VMEM_LIMIT = 48 * 1024 * 1024

MXU_DT = jnp.bfloat16
HI = lax.Precision.HIGHEST

HALF = 512
EVEN_IN = 4096
ODD_IN_PAD = 2944
ODD_ROPE = (2048, 2688)
EVEN_ROPE = (0, 1024)

PROJ_TM = 256
ATT_TQ = 256
SWA_TQ = 128
PREP_TR = 256

P_NSEL = P_HEADS * P_TOPK
ROUTE_TN = 256
APPLY_TB = 8
ROW_TILES = 2 * D_MODEL // LANES
ROW_GROUPS = P_NSEL // SUBLANES


def _cparams(n_axes=1):
    return pltpu.CompilerParams(dimension_semantics=("arbitrary",) * n_axes, vmem_limit_bytes=VMEM_LIMIT)


def _mm(a, b):
    return jnp.dot(a.astype(MXU_DT), b.astype(MXU_DT), preferred_element_type=jnp.float32)


def _mm_nt(a, b):
    return lax.dot_general(a.astype(MXU_DT), b.astype(MXU_DT), (((1,), (1,)), ((), ())),
                           preferred_element_type=jnp.float32)


def _mm_tn(a, b):
    return lax.dot_general(a.astype(MXU_DT), b.astype(MXU_DT), (((0,), (0,)), ((), ())),
                           preferred_element_type=jnp.float32)


def _mm_hi(a, b):
    return jnp.dot(a, b, precision=HI, preferred_element_type=jnp.float32)


def _mm_x3(a, b, dims=(((1,), (0,)), ((), ()))):
    a_hi, b_hi = a.astype(MXU_DT), b.astype(MXU_DT)
    a_lo = (a - a_hi.astype(jnp.float32)).astype(MXU_DT)
    b_lo = (b - b_hi.astype(jnp.float32)).astype(MXU_DT)
    dot = functools.partial(lax.dot_general, dimension_numbers=dims, preferred_element_type=jnp.float32)
    return dot(a_hi, b_hi) + (dot(a_hi, b_lo) + dot(a_lo, b_hi))


def _silu(x):
    return x * jax.nn.sigmoid(x)


def _ada_kernel(c_ref, w_ref, b_ref, o_ref):
    o_ref[...] = _mm(_silu(c_ref[...]), w_ref[...]) + b_ref[...]


def _ada_linear(cvecs, w, b):
    rows = cvecs.shape[0]
    return pl.pallas_call(
        _ada_kernel,
        grid=(ADA_CHUNKS,),
        in_specs=[pl.BlockSpec((rows, D_MODEL), lambda j: (0, 0)),
                  pl.BlockSpec((D_MODEL, D_MODEL), lambda j: (0, j)),
                  pl.BlockSpec((1, D_MODEL), lambda j: (0, j))],
        out_specs=pl.BlockSpec((rows, D_MODEL), lambda j: (0, j)),
        out_shape=jax.ShapeDtypeStruct((rows, ADA_CHUNKS * D_MODEL), jnp.float32),
        compiler_params=_cparams(), name="ada_linear",
    )(cvecs, w, b.reshape(1, -1))


def _inproj_kernel(x_ref, sh_ref, sc_ref, nw_ref, w_ref, *rest, rope):
    o_ref = rest[-1]
    x = x_ref[...]
    h = x * lax.rsqrt(jnp.mean(x * x, axis=-1, keepdims=True) + NORM_EPS) * nw_ref[...]
    h = h * (1.0 + sc_ref[0]) + sh_ref[0]
    y = jnp.dot(h.astype(MXU_DT), w_ref[...], preferred_element_type=jnp.float32)
    o_ref[...] = y
    if rope is not None:
        cos, sin_up, sin_dn = rest[0][...], rest[1][...], rest[2][...]
        for c in range(rope[0] // LANES, rope[1] // LANES):
            t = y[:, c * LANES:(c + 1) * LANES]
            o_ref[:, c * LANES:(c + 1) * LANES] = (t * cos + pltpu.roll(t, LANES - 16, 1) * sin_up
                                                   + pltpu.roll(t, 16, 1) * sin_dn)


def _inproj(x2, shift, scale, norm_w, w, tokens_per_batch, rope=None, rope_tabs=None):
    n_tok, f_out = x2.shape[0], w.shape[1]
    tpb = tokens_per_batch // PROJ_TM
    in_specs = [pl.BlockSpec((PROJ_TM, D_MODEL), lambda i: (i, 0)),
                pl.BlockSpec((1, 1, D_MODEL), lambda i: (i // tpb, 0, 0)),
                pl.BlockSpec((1, 1, D_MODEL), lambda i: (i // tpb, 0, 0)),
                pl.BlockSpec((1, D_MODEL), lambda i: (0, 0)),
                pl.BlockSpec((D_MODEL, f_out), lambda i: (0, 0))]
    args = [x2, shift, scale, norm_w.reshape(1, D_MODEL), w]
    if rope is not None:
        in_specs += [pl.BlockSpec((PROJ_TM, LANES), lambda i: (i % tpb, 0))] * 3
        args += list(rope_tabs)
    return pl.pallas_call(
        functools.partial(_inproj_kernel, rope=rope),
        grid=(n_tok // PROJ_TM,),
        in_specs=in_specs,
        out_specs=pl.BlockSpec((PROJ_TM, f_out), lambda i: (i, 0)),
        out_shape=jax.ShapeDtypeStruct((n_tok, f_out), jnp.float32),
        compiler_params=_cparams(), name="inproj",
    )(*args)


def _rope_tables(seq):
    rows = seq // GRID_W
    row = jnp.repeat(jnp.arange(rows, dtype=jnp.float32), GRID_W)
    col = jnp.tile(jnp.arange(GRID_W, dtype=jnp.float32), rows)
    inv = ROPE_THETA ** (-jnp.arange(ROPE_FREQS, dtype=jnp.float32) / ROPE_FREQS)
    ar = row[:, None] * inv[None, :]
    ac = col[:, None] * inv[None, :]
    ang = jnp.concatenate([ar, ar, ac, ac], axis=-1)
    cos = jnp.tile(jnp.cos(ang), (1, 2))
    sin = jnp.tile(jnp.sin(ang), (1, 2))
    quarter = (jnp.arange(LANES) % ROPE_DIM) // ROPE_FREQS
    sin_up = jnp.where(quarter % 2 == 0, -sin, 0.0)
    sin_dn = jnp.where(quarter % 2 == 1, sin, 0.0)
    return cos, sin_up, sin_dn


def _diffattn_kernel(lam_ref, q_ref, kc_ref, vc_ref, *rest, has_lat, out_scale):
    if has_lat:
        kl_ref, vl_ref, sw_ref, o_ref = rest
    else:
        sw_ref, o_ref = rest
    lam = lam_ref[0]
    q = q_ref[...] * (A_DH ** -0.5)
    lane = lax.broadcasted_iota(jnp.int32, (1, LANES), 1)
    first = lane < A_DH
    kc = kc_ref[...]
    exps = []
    for m in range(2):
        qm = jnp.where(first if m == 0 else jnp.logical_not(first), q, 0.0)
        s_c = _mm_nt(qm, kc)
        mx = jnp.max(s_c, axis=-1, keepdims=True)
        if has_lat:
            s_l = _mm_nt(qm, kl_ref[...])
            mx = jnp.maximum(mx, jnp.max(s_l, axis=-1, keepdims=True))
            e_l = jnp.exp(s_l - mx)
        e_c = jnp.exp(s_c - mx)
        den = jnp.sum(e_c, axis=-1, keepdims=True)
        if has_lat:
            den = den + jnp.sum(e_l, axis=-1, keepdims=True)
        exps.append((e_c, e_l if has_lat else None, 1.0 / den))
    w0, w1 = exps[0][2], lam * exps[1][2]
    o = _mm(exps[0][0] * w0 - exps[1][0] * w1, vc_ref[...])
    if has_lat:
        o = o + _mm(exps[0][1] * w0 - exps[1][1] * w1, vl_ref[...])
    o = o * lax.rsqrt(jnp.mean(o * o, axis=-1, keepdims=True) + NORM_EPS) * sw_ref[...]
    o_ref[...] = o * out_scale


def _diff_attention(lam, pq, pc, pl_lat, subln_w, n_q, n_ctx, n_lat, out_scale):
    batch = pq.shape[0] // n_q
    tq = min(ATT_TQ, n_q)
    nq = n_q // tq
    has_lat = pl_lat is not None
    in_specs = [pl.BlockSpec(memory_space=pltpu.SMEM),
                pl.BlockSpec((tq, LANES), lambda b, h, i: (b * nq + i, h)),
                pl.BlockSpec((n_ctx, LANES), lambda b, h, i: (b, A_HEADS + h)),
                pl.BlockSpec((n_ctx, LANES), lambda b, h, i: (b, 2 * A_HEADS + h))]
    args = [lam, pq, pc, pc]
    if has_lat:
        in_specs += [pl.BlockSpec((n_lat, LANES), lambda b, h, i: (b, A_HEADS + h)),
                     pl.BlockSpec((n_lat, LANES), lambda b, h, i: (b, 2 * A_HEADS + h))]
        args += [pl_lat, pl_lat]
    in_specs.append(pl.BlockSpec((1, LANES), lambda b, h, i: (0, 0)))
    args.append(subln_w.reshape(1, A_DV))
    return pl.pallas_call(
        functools.partial(_diffattn_kernel, has_lat=has_lat, out_scale=out_scale),
        grid=(batch, A_HEADS, nq),
        in_specs=in_specs,
        out_specs=pl.BlockSpec((tq, LANES), lambda b, h, i: (b * nq + i, h)),
        out_shape=jax.ShapeDtypeStruct((pq.shape[0], HALF), jnp.float32),
        compiler_params=_cparams(3), name="diff_attn",
    )(*args)


def _tri(n, upper):
    r = lax.broadcasted_iota(jnp.int32, (n, n), 0)
    c = lax.broadcasted_iota(jnp.int32, (n, n), 1)
    return (c >= r) if upper else (c <= r)


def _hgrn_dir(q_ref, i_ref, f_ref, lb, s_ref, o_ref, rev):
    n = B_CHUNK
    incl = _tri(n, rev)
    f = lb + (1.0 - lb) * jax.nn.sigmoid(f_ref[...])
    k = 1.0 - f
    g = jnp.log(f)
    qh = _silu(q_ref[...]) * (B_DK ** -0.5)
    v = i_ref[...]
    b = _mm_hi(incl.astype(jnp.float32), g)
    mid = n // 2
    ref_row = (n - 1 - mid) if rev else mid
    last_row = 0 if rev else n - 1
    ref = b[ref_row:ref_row + 1, :]
    b_last = b[last_row:last_row + 1, :]
    q_in = qh * jnp.exp(b - ref)
    k_in = k * jnp.exp(ref - b)
    q_st = qh * jnp.exp(b)
    k_st = k * jnp.exp(b_last - b)
    decay = jnp.exp(b_last)
    outs = []
    for h in range(B_HEADS):
        sl = slice(h * B_DK, (h + 1) * B_DK)
        st = s_ref[0, h]
        att = jnp.where(incl, _mm_nt(q_in[:, sl], k_in[:, sl]), 0.0)
        outs.append(_mm(att, v[:, sl]) + _mm_nt(q_st[:, sl], st))
        s_ref[0, h] = st * decay[:, sl] + _mm_tn(v[:, sl], k_st[:, sl])
    o_ref[...] = jnp.concatenate(outs, axis=-1)


def _hgrn_kernel(qf, if_, ff, qb, ib, fb, lb_ref, s0f, s0b, of, ob, sf, sb):
    @pl.when(pl.program_id(1) == 0)
    def _():
        sf[...] = s0f[...]
        sb[...] = s0b[...]
    lb = lb_ref[...]
    _hgrn_dir(qf, if_, ff, lb, sf, of, False)
    _hgrn_dir(qb, ib, fb, lb, sb, ob, True)


def _hgrn_scan(p, lb, s0f, s0b, n):
    batch = p.shape[0] // n
    nc = n // B_CHUNK
    blk = lambda col, rev: pl.BlockSpec(
        (B_CHUNK, HALF), (lambda b, c: (b * nc + nc - 1 - c, col)) if rev else (lambda b, c: (b * nc + c, col)))
    st = pl.BlockSpec((1, B_HEADS, B_DK, B_DV), lambda b, c: (b, 0, 0, 0))
    o_sd = jax.ShapeDtypeStruct((p.shape[0], HALF), jnp.float32)
    s_sd = jax.ShapeDtypeStruct((batch, B_HEADS, B_DK, B_DV), jnp.float32)
    return pl.pallas_call(
        _hgrn_kernel,
        grid=(batch, nc),
        in_specs=[blk(3, False), blk(4, False), blk(5, False), blk(3, True), blk(4, True), blk(6, True),
                  pl.BlockSpec((1, HALF), lambda b, c: (0, 0)), st, st],
        out_specs=[blk(0, False), blk(0, True), st, st],
        out_shape=[o_sd, o_sd, s_sd, s_sd],
        compiler_params=_cparams(2), name="hgrn_scan",
    )(p, p, p, p, p, p, lb.reshape(1, HALF), s0f, s0b)


def _gdn_prep_kernel(x_ref, prev_ref, next_ref, w_ref, o_ref, *, tiles_per_seq):
    i = pl.program_id(0)
    first = (i % tiles_per_seq) == 0
    last = (i % tiles_per_seq) == tiles_per_seq - 1
    x = x_ref[...]
    prev = jnp.where(first, 0.0, prev_ref[...])
    nxt = jnp.where(last, 0.0, next_ref[...])
    xe = jnp.concatenate([prev, x, nxt], axis=0)
    pad = C_CONV // 2
    acc = jnp.zeros_like(x)
    for j in range(C_CONV):
        acc = acc + xe[SUBLANES + j - pad:SUBLANES + j - pad + PREP_TR, :] * w_ref[j:j + 1, :]
    y = _silu(acc)
    for c in range(3 * C_HEADS):
        t = y[:, c * LANES:(c + 1) * LANES]
        if c < 2 * C_HEADS:
            t = t * lax.rsqrt(jnp.sum(t * t, axis=-1, keepdims=True) + NORM_EPS)
            if c < C_HEADS:
                t = t * (C_DK ** -0.5)
        o_ref[:, c * LANES:(c + 1) * LANES] = t


def _gdn_prep(p, conv_w, n):
    width = 3 * HALF
    nt = p.shape[0] // PREP_TR
    tps = n // PREP_TR
    rb = PREP_TR // SUBLANES
    last_blk = p.shape[0] // SUBLANES - 1
    return pl.pallas_call(
        functools.partial(_gdn_prep_kernel, tiles_per_seq=tps),
        grid=(nt,),
        in_specs=[pl.BlockSpec((PREP_TR, width), lambda i: (i, 0)),
                  pl.BlockSpec((SUBLANES, width), lambda i: (jnp.maximum(i * rb - 1, 0), 0)),
                  pl.BlockSpec((SUBLANES, width), lambda i: (jnp.minimum((i + 1) * rb, last_blk), 0)),
                  pl.BlockSpec((SUBLANES, width), lambda i: (0, 0))],
        out_specs=pl.BlockSpec((PREP_TR, width), lambda i: (i, 0)),
        out_shape=jax.ShapeDtypeStruct((p.shape[0], width), jnp.float32),
        compiler_params=_cparams(), name="gdn_prep",
    )(p, p, p, jnp.pad(conv_w, ((0, SUBLANES - C_CONV), (0, 0))))


def _gdn_dir(q_ref, k_ref, v_ref, gt_ref, nea, dtv, s_ref, o_ref, d, rev):
    n = C_CHUNK
    gates = gt_ref[...]
    x = gates + dtv
    la = nea * (jnp.maximum(x, 0.0) + jnp.log1p(jnp.exp(-jnp.abs(x))))
    beta_all = jax.nn.sigmoid(gates)
    inclf = _tri(n, rev).astype(jnp.float32)
    g_cols = _mm_hi(inclf, la)
    g_rows = lax.dot_general(la, inclf, (((0,), (1,)), ((), ())), precision=HI,
                             preferred_element_type=jnp.float32)
    last_row = 0 if rev else n - 1
    heads = range(C_HEADS)
    cols = [d * C_HEADS + h for h in heads]
    stack = lambda ref: jnp.concatenate([ref[:, h * LANES:(h + 1) * LANES] for h in heads], axis=0)
    q, k, v = stack(q_ref), stack(k_ref), stack(v_ref)
    gc = jnp.concatenate([g_cols[:, c:c + 1] for c in cols], axis=0)
    gr = jnp.concatenate([g_rows[c:c + 1, :] for c in cols], axis=1)
    beta = jnp.concatenate([beta_all[:, 2 * C_HEADS + c:2 * C_HEADS + c + 1] for c in cols], axis=0)
    rows = C_HEADS * n
    r = lax.broadcasted_iota(jnp.int32, (rows, rows), 0)
    c = lax.broadcasted_iota(jnp.int32, (rows, rows), 1)
    same = (r // n) == (c // n)
    incl = jnp.logical_and(same, (c >= r) if rev else (c <= r))
    strict = jnp.logical_and(same, (c > r) if rev else (c < r))
    eye = (r == c).astype(jnp.float32)
    decay = jnp.where(incl, jnp.exp(jnp.where(incl, gc - gr, 0.0)), 0.0)
    kb = k * beta
    nmat = jnp.where(strict, _mm_nt(kb, k) * decay, 0.0)
    inv = eye - nmat
    pw = _mm_x3(nmat, nmat)
    for step in range(5):
        inv = inv + _mm_x3(inv, pw)
        if step < 4:
            pw = _mm_x3(pw, pw)
    e_gc = jnp.exp(gc)
    rhs = jnp.concatenate([v * beta, kb * e_gc], axis=-1)
    sol = _mm_x3(inv, rhs)
    u, w = sol[:, :C_DV], sol[:, C_DV:]
    hs = lambda a, h: a[h * n:(h + 1) * n, :]
    states = [s_ref[0, h] for h in heads]
    v_new = jnp.concatenate([hs(u, h) - _mm(hs(w, h), states[h]) for h in heads], axis=0)
    att = _mm_nt(q, k) * decay
    o_in = _mm(att, v_new)
    q_st = q * e_gc
    o_ref[...] = jnp.concatenate([hs(o_in, h) + _mm(hs(q_st, h), states[h]) for h in heads], axis=-1)
    for h in heads:
        g_last = gc[h * n + last_row:h * n + last_row + 1, :]
        k_st = hs(k, h) * jnp.exp(g_last - hs(gc, h))
        s_ref[0, h] = states[h] * jnp.exp(g_last) + _mm_tn(k_st, hs(v_new, h))


def _gdn_kernel(qf, kf, vf, gf, qb, kb, vb, gb, nea_ref, dt_ref, s0f, s0b, of, ob, sf, sb):
    @pl.when(pl.program_id(1) == 0)
    def _():
        sf[...] = s0f[...]
        sb[...] = s0b[...]
    nea, dtv = nea_ref[...], dt_ref[...]
    _gdn_dir(qf, kf, vf, gf, nea, dtv, sf, of, 0, False)
    _gdn_dir(qb, kb, vb, gb, nea, dtv, sb, ob, 1, True)


def _gdn_scan(qkv, p, nea, dtv, s0f, s0b, n):
    batch = p.shape[0] // n
    nc = n // C_CHUNK
    gate_col = (ODD_IN_PAD - LANES) // LANES

    def blk(width, col, rev):
        return pl.BlockSpec((C_CHUNK, width),
                            (lambda b, c: (b * nc + nc - 1 - c, col)) if rev else (lambda b, c: (b * nc + c, col)))
    st = pl.BlockSpec((1, C_HEADS, C_DK, C_DV), lambda b, c: (b, 0, 0, 0))
    vec = pl.BlockSpec((1, LANES), lambda b, c: (0, 0))
    o_sd = jax.ShapeDtypeStruct((p.shape[0], HALF), jnp.float32)
    s_sd = jax.ShapeDtypeStruct((batch, C_HEADS, C_DK, C_DV), jnp.float32)
    return pl.pallas_call(
        _gdn_kernel,
        grid=(batch, nc),
        in_specs=[blk(HALF, 0, False), blk(HALF, 1, False), blk(HALF, 2, False), blk(LANES, gate_col, False),
                  blk(HALF, 0, True), blk(HALF, 1, True), blk(HALF, 2, True), blk(LANES, gate_col, True),
                  vec, vec, st, st],
        out_specs=[blk(HALF, 0, False), blk(HALF, 0, True), st, st],
        out_shape=[o_sd, o_sd, s_sd, s_sd],
        compiler_params=_cparams(2), name="gdn_scan",
    )(qkv, qkv, qkv, p, qkv, qkv, qkv, p, nea, dtv, s0f, s0b)


def _swa_kernel(sink_ref, q_ref, kp_ref, k0_ref, kn_ref, vp_ref, v0_ref, vn_ref, kc_ref, vc_ref, o_ref, *, nblk):
    i = pl.program_id(1)
    tq = SWA_TQ
    q = q_ref[...] * (D_DH ** -0.5)
    kband = jnp.concatenate([kp_ref[...], k0_ref[...], kn_ref[...]], axis=0)
    vband = jnp.concatenate([vp_ref[...], v0_ref[...], vn_ref[...]], axis=0)
    qpos = lax.broadcasted_iota(jnp.int32, (tq, 3 * tq), 0)
    kpos = lax.broadcasted_iota(jnp.int32, (tq, 3 * tq), 1) - tq
    ok = jnp.abs(kpos - qpos) <= D_WINDOW
    ok = jnp.logical_and(ok, jnp.logical_or(kpos >= 0, i > 0))
    ok = jnp.logical_and(ok, jnp.logical_or(kpos < tq, i < nblk - 1))
    kc, vc = kc_ref[...], vc_ref[...]
    group = D_HEADS // D_KV_HEADS
    outs = []
    for h in range(D_HEADS):
        ks = slice((h // group) * D_DH, (h // group + 1) * D_DH)
        qh = q[:, h * D_DH:(h + 1) * D_DH]
        s_b = jnp.where(ok, _mm_nt(qh, kband[:, ks]), -jnp.inf)
        s_c = _mm_nt(qh, kc[:, ks])
        sink = sink_ref[h]
        mx = jnp.maximum(jnp.maximum(jnp.max(s_b, axis=-1, keepdims=True), jnp.max(s_c, axis=-1, keepdims=True)), sink)
        e_b = jnp.exp(s_b - mx)
        e_c = jnp.exp(s_c - mx)
        den = jnp.exp(sink - mx) + jnp.sum(e_b, axis=-1, keepdims=True) + jnp.sum(e_c, axis=-1, keepdims=True)
        inv = 1.0 / den
        outs.append(_mm(e_c * inv, vc[:, ks]) + _mm(e_b * inv, vband[:, ks]))
    o_ref[...] = jnp.concatenate(outs, axis=-1)


def _window_attention(sink, p, pc, n_lat, n_ctx):
    batch = p.shape[0] // n_lat
    nblk = n_lat // SWA_TQ
    qcol = ODD_ROPE[0] // HALF
    kcol = (ODD_ROPE[0] + HALF) // LANES
    vcol = kcol + 1
    prev = lambda col: pl.BlockSpec((SWA_TQ, LANES), lambda b, i: (b * nblk + jnp.maximum(i - 1, 0), col))
    cur = lambda col: pl.BlockSpec((SWA_TQ, LANES), lambda b, i: (b * nblk + i, col))
    nxt = lambda col: pl.BlockSpec((SWA_TQ, LANES), lambda b, i: (b * nblk + jnp.minimum(i + 1, nblk - 1), col))
    return pl.pallas_call(
        functools.partial(_swa_kernel, nblk=nblk),
        grid=(batch, nblk),
        in_specs=[pl.BlockSpec(memory_space=pltpu.SMEM),
                  pl.BlockSpec((SWA_TQ, HALF), lambda b, i: (b * nblk + i, qcol)),
                  prev(kcol), cur(kcol), nxt(kcol), prev(vcol), cur(vcol), nxt(vcol),
                  pl.BlockSpec((n_ctx, LANES), lambda b, i: (b, kcol)),
                  pl.BlockSpec((n_ctx, LANES), lambda b, i: (b, vcol))],
        out_specs=pl.BlockSpec((SWA_TQ, HALF), lambda b, i: (b * nblk + i, 0)),
        out_shape=jax.ShapeDtypeStruct((p.shape[0], HALF), jnp.float32),
        compiler_params=_cparams(2), name="window_attn",
    )(sink, p, p, p, p, p, p, p, pc, pc)


def _outproj_kernel(att_ref, of_ref, ob_ref, g_ref, nw_ref, w_ref, x_ref, gate_ref, o_ref, *, group, rec_first):
    o = of_ref[...] + ob_ref[...]
    sq = o * o
    if group == LANES:
        parts = [jnp.broadcast_to(jnp.mean(sq[:, c * LANES:(c + 1) * LANES], axis=-1, keepdims=True),
                                  (sq.shape[0], LANES)) for c in range(HALF // LANES)]
        ms = jnp.concatenate(parts, axis=-1)
    else:
        r = lax.broadcasted_iota(jnp.int32, (HALF, HALF), 0) // group
        c = lax.broadcasted_iota(jnp.int32, (HALF, HALF), 1) // group
        ms = _mm_hi(sq, (r == c).astype(jnp.float32)) * (1.0 / group)
    rec = o * lax.rsqrt(ms + NORM_EPS) * nw_ref[...] * _silu(g_ref[...])
    att = att_ref[...]
    top, bot = (rec, att) if rec_first else (att, rec)
    y = _mm(top, w_ref[:HALF, :]) + _mm(bot, w_ref[HALF:, :])
    o_ref[...] = x_ref[...] + gate_ref[0] * y


def _outproj(att, of, ob, p, gate_col, norm_w, w_out, x2, gate, tokens_per_batch, group, rec_first):
    n_tok = x2.shape[0]
    tpb = tokens_per_batch // PROJ_TM
    row = lambda col: pl.BlockSpec((PROJ_TM, HALF), lambda i: (i, col))
    return pl.pallas_call(
        functools.partial(_outproj_kernel, group=group, rec_first=rec_first),
        grid=(n_tok // PROJ_TM,),
        in_specs=[row(0), row(0), row(0), row(gate_col),
                  pl.BlockSpec((1, HALF), lambda i: (0, 0)),
                  pl.BlockSpec((2 * HALF, D_MODEL), lambda i: (0, 0)),
                  pl.BlockSpec((PROJ_TM, D_MODEL), lambda i: (i, 0)),
                  pl.BlockSpec((1, 1, D_MODEL), lambda i: (i // tpb, 0, 0))],
        out_specs=pl.BlockSpec((PROJ_TM, D_MODEL), lambda i: (i, 0)),
        out_shape=jax.ShapeDtypeStruct((n_tok, D_MODEL), jnp.float32),
        compiler_params=_cparams(), name="outproj",
    )(att, of, ob, p, norm_w.reshape(1, HALF), w_out, x2, gate)


def _top16_rows(s, payload=None):
    rows = s.shape[0]
    iota = lax.broadcasted_iota(jnp.int32, s.shape, 0)
    vals, poss, pays = [], [], []
    for _ in range(P_TOPK):
        m = jnp.max(s, axis=0, keepdims=True)
        pos = jnp.min(jnp.where(s == m, iota, rows), axis=0, keepdims=True)
        hit = iota == pos
        if payload is not None:
            pays.append(jnp.max(jnp.where(hit, payload, -1), axis=0, keepdims=True))
        vals.append(m)
        poss.append(pos)
        s = jnp.where(hit, -jnp.inf, s)
    cat = lambda xs: jnp.concatenate(xs, axis=0)
    return cat(vals), cat(poss), (cat(pays) if payload is not None else None)


def _route_kernel(x_ref, sh_ref, sc_ref, nw_ref, wqt_ref, keys_ref, h_ref, idx_ref, gate_ref):
    x = x_ref[...]
    h = x * lax.rsqrt(jnp.mean(x * x, axis=-1, keepdims=True) + NORM_EPS) * nw_ref[...]
    h = h * (1.0 + sc_ref[0]) + sh_ref[0]
    h_ref[...] = h
    qt = _mm_x3(wqt_ref[...], h, (((1,), (1,)), ((), ())))
    idx_rows, gate_rows = [], []
    for hh in range(P_HEADS):
        tops = []
        for p in range(2):
            j = hh * 2 + p
            s = _mm_x3(keys_ref[j], qt[j * P_NKEYS:(j + 1) * P_NKEYS, :])
            v, i, _ = _top16_rows(s)
            tops.append((v, i))
        (v0, i0), (v1, i1) = tops
        nb = [P_TOPK // (a + 1) for a in range(P_TOPK)]
        n_pad = -sum(nb) % SUBLANES
        cand = jnp.concatenate([v0[a:a + 1, :] + v1[:nb[a], :] for a in range(P_TOPK)]
                               + [jnp.full((n_pad, v0.shape[1]), -jnp.inf, jnp.float32)], axis=0)
        cidx = jnp.concatenate([i0[a:a + 1, :] * P_NKEYS + i1[:nb[a], :] for a in range(P_TOPK)]
                               + [jnp.zeros((n_pad, v0.shape[1]), jnp.int32)], axis=0)
        best, _, sel = _top16_rows(cand, cidx)
        e = jnp.exp(best - best[0:1, :])
        gate_rows.append(e / jnp.sum(e, axis=0, keepdims=True))
        idx_rows.append(sel)
    idx_t = jnp.concatenate(idx_rows, axis=0)
    gate_t = jnp.concatenate(gate_rows, axis=0)
    idx_ref[...] = idx_t.T
    for j in range(ROUTE_TN // APPLY_TB):
        gate_ref[j] = gate_t[:, j * APPLY_TB:(j + 1) * APPLY_TB]


def _peer_route(x2, shift, scale, norm_w, wq_t, keys, tokens_per_batch):
    n_tok = x2.shape[0]
    tiles_per_batch = tokens_per_batch // ROUTE_TN
    nkeys2 = 2 * P_HEADS
    return pl.pallas_call(
        _route_kernel,
        grid=(n_tok // ROUTE_TN,),
        in_specs=[
            pl.BlockSpec((ROUTE_TN, D_MODEL), lambda i: (i, 0)),
            pl.BlockSpec((1, 1, D_MODEL), lambda i: (i // tiles_per_batch, 0, 0)),
            pl.BlockSpec((1, 1, D_MODEL), lambda i: (i // tiles_per_batch, 0, 0)),
            pl.BlockSpec((1, D_MODEL), lambda i: (0, 0)),
            pl.BlockSpec((P_HEADS * P_DQ, D_MODEL), lambda i: (0, 0)),
            pl.BlockSpec((nkeys2, P_NKEYS, P_DQ // 2), lambda i: (0, 0, 0)),
        ],
        out_specs=[
            pl.BlockSpec((ROUTE_TN, D_MODEL), lambda i: (i, 0)),
            pl.BlockSpec((ROUTE_TN, P_NSEL), lambda i: (i, 0)),
            pl.BlockSpec((ROUTE_TN // APPLY_TB, P_NSEL, APPLY_TB), lambda i: (i, 0, 0)),
        ],
        out_shape=[
            jax.ShapeDtypeStruct((n_tok, D_MODEL), jnp.float32),
            jax.ShapeDtypeStruct((n_tok, P_NSEL), jnp.int32),
            jax.ShapeDtypeStruct((n_tok // APPLY_TB, P_NSEL, APPLY_TB), jnp.float32),
        ],
        compiler_params=_cparams(), name="peer_route",
    )(x2, shift, scale, norm_w.reshape(1, D_MODEL), wq_t, keys.reshape(nkeys2, P_NKEYS, P_DQ // 2))


def _apply_kernel(idx_cur, idx_nxt, h_ref, gate_ref, x_ref, g2_ref, tab_ref, o_ref, buf, sem):
    i = pl.program_id(0)
    n = pl.num_programs(0)
    slot = i % 2

    def row_copy(idx_ref, s, t, r):
        return pltpu.make_async_copy(
            tab_ref.at[idx_ref[t, r]],
            buf.at[s, t * ROW_GROUPS + r // SUBLANES, :, r % SUBLANES, :],
            sem.at[s])

    def issue(idx_ref, s):
        def body(t, carry):
            for r in range(P_NSEL):
                row_copy(idx_ref, s, t, r).start(priority=r % 2)
            return carry
        lax.fori_loop(0, APPLY_TB, body, 0)

    @pl.when(i == 0)
    def _():
        issue(idx_cur, 0)

    @pl.when(i + 1 < n)
    def _():
        issue(idx_nxt, 1 - slot)

    pltpu.make_async_copy(buf.at[slot], buf.at[slot], sem.at[slot]).wait()

    half = ROW_TILES // 2
    lane = lax.broadcasted_iota(jnp.int32, (P_NSEL, APPLY_TB), 1)
    a = jnp.zeros((P_NSEL, APPLY_TB), jnp.float32)
    for t in range(APPLY_TB):
        u = buf[slot, t * ROW_GROUPS:(t + 1) * ROW_GROUPS, :half]
        xt = h_ref[t:t + 1, :].reshape(1, half, 1, LANES)
        s = jnp.sum(u * xt, axis=1)
        at = jnp.sum(s, axis=-1, keepdims=True).reshape(P_NSEL, 1)
        a = jnp.where(lane == t, at, a)
    coef = gate_ref[0] * (0.5 * a * (1.0 + lax.erf(a * (2.0 ** -0.5))))
    for t in range(APPLY_TB):
        v = buf[slot, t * ROW_GROUPS:(t + 1) * ROW_GROUPS, half:]
        ct = coef[:, t:t + 1].reshape(ROW_GROUPS, 1, SUBLANES, 1)
        o = jnp.sum(jnp.sum(ct * v, axis=0), axis=1)
        o_ref[t:t + 1, :] = x_ref[t:t + 1, :] + g2_ref[0] * o.reshape(1, D_MODEL)


def _peer_apply(idx, h, gate_t, x2, g2, tab, tokens_per_batch):
    n_tok = h.shape[0]
    nt = n_tok // APPLY_TB
    tpb = tokens_per_batch // APPLY_TB
    return pl.pallas_call(
        _apply_kernel,
        grid=(nt,),
        in_specs=[
            pl.BlockSpec((APPLY_TB, P_NSEL), lambda i: (i, 0), memory_space=pltpu.SMEM),
            pl.BlockSpec((APPLY_TB, P_NSEL), lambda i: (jnp.minimum(i + 1, nt - 1), 0), memory_space=pltpu.SMEM),
            pl.BlockSpec((APPLY_TB, D_MODEL), lambda i: (i, 0)),
            pl.BlockSpec((1, P_NSEL, APPLY_TB), lambda i: (i, 0, 0)),
            pl.BlockSpec((APPLY_TB, D_MODEL), lambda i: (i, 0)),
            pl.BlockSpec((1, 1, D_MODEL), lambda i: (i // tpb, 0, 0)),
            pl.BlockSpec(memory_space=pl.ANY),
        ],
        out_specs=pl.BlockSpec((APPLY_TB, D_MODEL), lambda i: (i, 0)),
        out_shape=jax.ShapeDtypeStruct((n_tok, D_MODEL), jnp.float32),
        scratch_shapes=[pltpu.VMEM((2, APPLY_TB * ROW_GROUPS, ROW_TILES, SUBLANES, LANES), jnp.float32),
                        pltpu.SemaphoreType.DMA((2,))],
        compiler_params=_cparams(), name="peer_apply",
    )(idx, idx, h, gate_t, x2, g2, tab)


def _peer_block(x2, norm_w, shift, scale, g2, wq_t, keys, tab, tokens_per_batch):
    h, idx, gate_t = _peer_route(x2, shift, scale, norm_w, wq_t, keys, tokens_per_batch)
    return _peer_apply(idx, h, gate_t, x2, g2, tab, tokens_per_batch)


def _final_norm_kernel(x_ref, w_ref, o_ref):
    x = x_ref[...]
    o_ref[...] = x * lax.rsqrt(jnp.mean(x * x, axis=-1, keepdims=True) + NORM_EPS) * w_ref[...]


def _final_norm(x2, w):
    tm = 512
    return pl.pallas_call(
        _final_norm_kernel,
        grid=(x2.shape[0] // tm,),
        in_specs=[pl.BlockSpec((tm, D_MODEL), lambda i: (i, 0)), pl.BlockSpec((1, D_MODEL), lambda i: (0, 0))],
        out_specs=pl.BlockSpec((tm, D_MODEL), lambda i: (i, 0)),
        out_shape=jax.ShapeDtypeStruct(x2.shape, jnp.float32),
        compiler_params=_cparams(), name="final_norm",
    )(x2, w.reshape(1, D_MODEL))


def _even_layer(layer, x2, xc2, mods, modc, norm_w, w_in, w_out, diff_lambda, subln_w, hgrn_lb, hgrn_norm_w,
                rope_tabs, T, L, need_ctx):
    B = x2.shape[0] // T
    sh1, sc1, g1 = mods
    sh1c, sc1c, g1c = modc
    w_in_b = w_in.astype(MXU_DT)
    p = _inproj(x2, sh1, sc1, norm_w, w_in_b, T, rope=EVEN_ROPE, rope_tabs=rope_tabs)
    pc = _inproj(xc2, sh1c, sc1c, norm_w, w_in_b, L)

    lambda_init = LAMBDA_INIT_BASE - LAMBDA_INIT_SPAN * math.exp(-LAMBDA_INIT_RATE * layer)
    lam = (jnp.exp(jnp.sum(diff_lambda[0] * diff_lambda[1])) - jnp.exp(jnp.sum(diff_lambda[2] * diff_lambda[3]))
           + lambda_init).reshape(1)
    o_a = _diff_attention(lam, p, pc, p, subln_w, T, L, T, 1.0 - lambda_init)

    lb = jnp.cumsum(jax.nn.softmax(hgrn_lb, axis=0), axis=0)[layer]
    s0 = jnp.zeros((B, B_HEADS, B_DK, B_DV), jnp.float32)
    oc_f, oc_b, sf, sb = _hgrn_scan(pc, lb, s0, s0, L)
    o_f, o_b, _, _ = _hgrn_scan(p, lb, sf, sb, T)

    w_out_b = w_out.astype(MXU_DT)
    nw_t = jnp.tile(hgrn_norm_w, B_HEADS)
    x2 = _outproj(o_a, o_f, o_b, p, 7, nw_t, w_out_b, x2, g1, T, B_DV, False)
    if need_ctx:
        oc_a = _diff_attention(lam, pc, pc, None, subln_w, L, L, 0, 1.0 - lambda_init)
        xc2 = _outproj(oc_a, oc_f, oc_b, pc, 7, nw_t, w_out_b, xc2, g1c, L, B_DV, False)
    return x2, xc2


def _odd_layer(x2, xc2, mods, modc, norm_w, w_in, w_out, conv_w, a_log, dt_bias, gdn_norm_w, sink, rope_tabs, T, L):
    B = x2.shape[0] // T
    sh1, sc1, g1 = mods
    sh1c, sc1c, _ = modc
    c_qkv, c_z, c_g, c_q, c_kv = np.cumsum((0, C_HEADS * (2 * C_DK + C_DV), C_HEADS * C_DV, 4 * C_HEADS,
                                            D_HEADS * D_DH))[:5]
    end = c_kv + 2 * D_KV_HEADS * D_DH
    w_r = jnp.concatenate([w_in[:, c_qkv:c_g], w_in[:, c_q:end], w_in[:, c_g:c_q],
                           jnp.zeros((D_MODEL, ODD_IN_PAD - end), w_in.dtype)], axis=1).astype(MXU_DT)
    p = _inproj(x2, sh1, sc1, norm_w, w_r, T, rope=ODD_ROPE, rope_tabs=rope_tabs)
    pc = _inproj(xc2, sh1c, sc1c, norm_w, w_r, L)

    qkv = _gdn_prep(p, conv_w, T)
    qkvc = _gdn_prep(pc, conv_w, L)
    pad = lambda v: jnp.pad(v.reshape(1, -1), ((0, 0), (0, LANES - 2 * C_HEADS)))
    nea = pad(-jnp.exp(a_log))
    dtv = pad(dt_bias)
    s0 = jnp.zeros((B, C_HEADS, C_DK, C_DV), jnp.float32)
    _, _, sf, sb = _gdn_scan(qkvc, pc, nea, dtv, s0, s0, L)
    o_f, o_b, _, _ = _gdn_scan(qkv, p, nea, dtv, sf, sb, T)

    o_d = _window_attention(sink, p, pc, T, L)
    x2 = _outproj(o_d, o_f, o_b, p, 3, jnp.tile(gdn_norm_w, C_HEADS), w_out.astype(MXU_DT), x2, g1, T, C_DV, True)
    return x2


def kernel(x, c, ctx, c_ctx, ada_w, ada_b, norm_w, final_norm_w, even_w_in, even_w_out, diff_lambda, diff_subln_w, hgrn_lb, hgrn_norm_w, odd_w_in, odd_w_out, gdn_conv_w, gdn_a_log, gdn_dt_bias, gdn_norm_w, swa_sink, peer_w_q, peer_keys, peer_u, peer_v):
    B, T, D = x.shape
    L = ctx.shape[1]
    rope_tabs = _rope_tables(T)
    x2 = x.reshape(B * T, D)
    xc2 = ctx.reshape(B * L, D)
    cvecs = jnp.concatenate([c, jnp.broadcast_to(c_ctx[None, :], (SUBLANES, D))], axis=0)
    for layer in range(DEPTH):
        need_ctx = layer < DEPTH - 1
        ada = _ada_linear(cvecs, ada_w[layer], ada_b[layer])
        lat = [ada[:B, j * D:(j + 1) * D].reshape(B, 1, D) for j in range(ADA_CHUNKS)]
        cx = [jnp.broadcast_to(ada[B:B + 1, j * D:(j + 1) * D].reshape(1, 1, D), (B, 1, D)) for j in range(ADA_CHUNKS)]
        if layer % 2 == 0:
            e = layer // 2
            x2, xc2 = _even_layer(layer, x2, xc2, lat[:3], cx[:3], norm_w[layer, 0], even_w_in[e], even_w_out[e],
                                  diff_lambda[e], diff_subln_w[e], hgrn_lb, hgrn_norm_w[e], rope_tabs, T, L, need_ctx)
        else:
            o = layer // 2
            x2 = _odd_layer(x2, xc2, lat[:3], cx[:3], norm_w[layer, 0], odd_w_in[o], odd_w_out[o], gdn_conv_w[o],
                            gdn_a_log[o], gdn_dt_bias[o], gdn_norm_w[o], swa_sink[o], rope_tabs, T, L)
        wq_t = peer_w_q[layer].T
        tab = jnp.concatenate([peer_u[layer], peer_v[layer]], axis=-1).reshape(-1, ROW_TILES, LANES)
        x2 = _peer_block(x2, norm_w[layer, 1], lat[3], lat[4], lat[5], wq_t, peer_keys[layer], tab, T)
        if need_ctx:
            xc2 = _peer_block(xc2, norm_w[layer, 1], cx[3], cx[4], cx[5], wq_t, peer_keys[layer], tab, L)
    return _final_norm(x2, final_norm_w).reshape(B, T, D)
```

```python
import functools
import math

import jax
import jax.numpy as jnp
import numpy as np
from jax import lax
from jax.experimental import pallas as pl
from jax.experimental.pallas import tpu as pltpu

D_MODEL = 1024
DEPTH = 2
GRID_W = 64
ADA_CHUNKS = 6
NORM_EPS = 1e-6
ROPE_THETA = 10000.0
ROPE_DIM = 64
ROPE_FREQS = ROPE_DIM // 4

A_HEADS = 4
A_DH = 64
A_DV = 2 * A_DH
LAMBDA_INIT_BASE = 0.8
LAMBDA_INIT_SPAN = 0.6
LAMBDA_INIT_RATE = 0.3
B_HEADS = 8
B_DK = 64
B_DV = 64
B_CHUNK = 32
C_HEADS = 4
C_DK = 128
C_DV = 128
C_CHUNK = 64
C_CONV = 5
D_HEADS = 8
D_KV_HEADS = 2
D_DH = 64
D_WINDOW = 128
P_HEADS = 8
P_NKEYS = 128
P_DQ = 256
P_TOPK = 16

SUBLANES = 8
LANES = 128
VMEM_LIMIT = 48 * 1024 * 1024

MXU_DT = jnp.bfloat16
HI = lax.Precision.HIGHEST

HALF = 512
EVEN_IN = 4096
ODD_IN_PAD = 2944
ODD_ROPE = (2048, 2688)
EVEN_ROPE = (0, 1024)

PROJ_TM = 256
ATT_TQ = 256
SWA_TQ = 128
PREP_TR = 256
SCAN_NB = 2

P_NSEL = P_HEADS * P_TOPK
ROUTE_TN = 256
APPLY_TB = 8
ROW_TILES = 2 * D_MODEL // LANES
ROW_GROUPS = P_NSEL // SUBLANES


def _cparams(n_axes=1):
    return pltpu.CompilerParams(dimension_semantics=("arbitrary",) * n_axes, vmem_limit_bytes=VMEM_LIMIT)


def _mm(a, b):
    return jnp.dot(a.astype(MXU_DT), b.astype(MXU_DT), preferred_element_type=jnp.float32)


def _mm_nt(a, b):
    return lax.dot_general(a.astype(MXU_DT), b.astype(MXU_DT), (((1,), (1,)), ((), ())),
                           preferred_element_type=jnp.float32)


def _mm_tn(a, b):
    return lax.dot_general(a.astype(MXU_DT), b.astype(MXU_DT), (((0,), (0,)), ((), ())),
                           preferred_element_type=jnp.float32)


def _mm_hi(a, b):
    return jnp.dot(a, b, precision=HI, preferred_element_type=jnp.float32)


def _mm_x3(a, b, dims=(((1,), (0,)), ((), ()))):
    a_hi, b_hi = a.astype(MXU_DT), b.astype(MXU_DT)
    a_lo = (a - a_hi.astype(jnp.float32)).astype(MXU_DT)
    b_lo = (b - b_hi.astype(jnp.float32)).astype(MXU_DT)
    dot = functools.partial(lax.dot_general, dimension_numbers=dims, preferred_element_type=jnp.float32)
    return dot(a_hi, b_hi) + (dot(a_hi, b_lo) + dot(a_lo, b_hi))


def _silu(x):
    return x * jax.nn.sigmoid(x)


def _ada_kernel(c_ref, w_ref, b_ref, o_ref):
    o_ref[...] = _mm(_silu(c_ref[...]), w_ref[...]) + b_ref[...]


def _ada_linear(cvecs, w, b):
    rows = cvecs.shape[0]
    return pl.pallas_call(
        _ada_kernel,
        grid=(ADA_CHUNKS,),
        in_specs=[pl.BlockSpec((rows, D_MODEL), lambda j: (0, 0)),
                  pl.BlockSpec((D_MODEL, D_MODEL), lambda j: (0, j)),
                  pl.BlockSpec((1, D_MODEL), lambda j: (0, j))],
        out_specs=pl.BlockSpec((rows, D_MODEL), lambda j: (0, j)),
        out_shape=jax.ShapeDtypeStruct((rows, ADA_CHUNKS * D_MODEL), jnp.float32),
        compiler_params=_cparams(), name="ada_linear",
    )(cvecs, w, b.reshape(1, -1))


def _inproj_kernel(x_ref, sh_ref, sc_ref, nw_ref, w_ref, *rest, rope):
    o_ref = rest[-1]
    x = x_ref[...]
    h = x * lax.rsqrt(jnp.mean(x * x, axis=-1, keepdims=True) + NORM_EPS) * nw_ref[...]
    h = h * (1.0 + sc_ref[0]) + sh_ref[0]
    y = jnp.dot(h.astype(MXU_DT), w_ref[...], preferred_element_type=jnp.float32)
    o_ref[...] = y
    if rope is not None:
        cos, sin_up, sin_dn = rest[0][...], rest[1][...], rest[2][...]
        for c in range(rope[0] // LANES, rope[1] // LANES):
            t = y[:, c * LANES:(c + 1) * LANES]
            o_ref[:, c * LANES:(c + 1) * LANES] = (t * cos + pltpu.roll(t, LANES - 16, 1) * sin_up
                                                   + pltpu.roll(t, 16, 1) * sin_dn)


def _inproj(x2, shift, scale, norm_w, w, tokens_per_batch, rope=None, rope_tabs=None):
    n_tok, f_out = x2.shape[0], w.shape[1]
    tpb = tokens_per_batch // PROJ_TM
    in_specs = [pl.BlockSpec((PROJ_TM, D_MODEL), lambda i: (i, 0)),
                pl.BlockSpec((1, 1, D_MODEL), lambda i: (i // tpb, 0, 0)),
                pl.BlockSpec((1, 1, D_MODEL), lambda i: (i // tpb, 0, 0)),
                pl.BlockSpec((1, D_MODEL), lambda i: (0, 0)),
                pl.BlockSpec((D_MODEL, f_out), lambda i: (0, 0))]
    args = [x2, shift, scale, norm_w.reshape(1, D_MODEL), w]
    if rope is not None:
        in_specs += [pl.BlockSpec((PROJ_TM, LANES), lambda i: (i % tpb, 0))] * 3
        args += list(rope_tabs)
    return pl.pallas_call(
        functools.partial(_inproj_kernel, rope=rope),
        grid=(n_tok // PROJ_TM,),
        in_specs=in_specs,
        out_specs=pl.BlockSpec((PROJ_TM, f_out), lambda i: (i, 0)),
        out_shape=jax.ShapeDtypeStruct((n_tok, f_out), jnp.float32),
        compiler_params=_cparams(), name="inproj",
    )(*args)


def _rope_tables(seq):
    rows = seq // GRID_W
    row = jnp.repeat(jnp.arange(rows, dtype=jnp.float32), GRID_W)
    col = jnp.tile(jnp.arange(GRID_W, dtype=jnp.float32), rows)
    inv = ROPE_THETA ** (-jnp.arange(ROPE_FREQS, dtype=jnp.float32) / ROPE_FREQS)
    ar = row[:, None] * inv[None, :]
    ac = col[:, None] * inv[None, :]
    ang = jnp.concatenate([ar, ar, ac, ac], axis=-1)
    cos = jnp.tile(jnp.cos(ang), (1, 2))
    sin = jnp.tile(jnp.sin(ang), (1, 2))
    quarter = (jnp.arange(LANES) % ROPE_DIM) // ROPE_FREQS
    sin_up = jnp.where(quarter % 2 == 0, -sin, 0.0)
    sin_dn = jnp.where(quarter % 2 == 1, sin, 0.0)
    return cos, sin_up, sin_dn


def _diffattn_kernel(lam_ref, q_ref, kc_ref, vc_ref, *rest, has_lat, out_scale):
    if has_lat:
        kl_ref, vl_ref, sw_ref, o_ref = rest
    else:
        sw_ref, o_ref = rest
    lam = lam_ref[0]
    q = q_ref[...] * (A_DH ** -0.5)
    lane = lax.broadcasted_iota(jnp.int32, (1, LANES), 1)
    first = lane < A_DH
    kc = kc_ref[...]
    exps = []
    for m in range(2):
        qm = jnp.where(first if m == 0 else jnp.logical_not(first), q, 0.0)
        s_c = _mm_nt(qm, kc)
        mx = jnp.max(s_c, axis=-1, keepdims=True)
        if has_lat:
            s_l = _mm_nt(qm, kl_ref[...])
            mx = jnp.maximum(mx, jnp.max(s_l, axis=-1, keepdims=True))
            e_l = jnp.exp(s_l - mx)
        e_c = jnp.exp(s_c - mx)
        den = jnp.sum(e_c, axis=-1, keepdims=True)
        if has_lat:
            den = den + jnp.sum(e_l, axis=-1, keepdims=True)
        exps.append((e_c, e_l if has_lat else None, 1.0 / den))
    w0, w1 = exps[0][2], lam * exps[1][2]
    o = _mm(exps[0][0] * w0 - exps[1][0] * w1, vc_ref[...])
    if has_lat:
        o = o + _mm(exps[0][1] * w0 - exps[1][1] * w1, vl_ref[...])
    o = o * lax.rsqrt(jnp.mean(o * o, axis=-1, keepdims=True) + NORM_EPS) * sw_ref[...]
    o_ref[...] = o * out_scale


def _diff_attention(lam, pq, pc, pl_lat, subln_w, n_q, n_ctx, n_lat, out_scale):
    batch = pq.shape[0] // n_q
    tq = min(ATT_TQ, n_q)
    nq = n_q // tq
    has_lat = pl_lat is not None
    in_specs = [pl.BlockSpec(memory_space=pltpu.SMEM),
                pl.BlockSpec((tq, LANES), lambda b, h, i: (b * nq + i, h)),
                pl.BlockSpec((n_ctx, LANES), lambda b, h, i: (b, A_HEADS + h)),
                pl.BlockSpec((n_ctx, LANES), lambda b, h, i: (b, 2 * A_HEADS + h))]
    args = [lam, pq, pc, pc]
    if has_lat:
        in_specs += [pl.BlockSpec((n_lat, LANES), lambda b, h, i: (b, A_HEADS + h)),
                     pl.BlockSpec((n_lat, LANES), lambda b, h, i: (b, 2 * A_HEADS + h))]
        args += [pl_lat, pl_lat]
    in_specs.append(pl.BlockSpec((1, LANES), lambda b, h, i: (0, 0)))
    args.append(subln_w.reshape(1, A_DV))
    return pl.pallas_call(
        functools.partial(_diffattn_kernel, has_lat=has_lat, out_scale=out_scale),
        grid=(batch, A_HEADS, nq),
        in_specs=in_specs,
        out_specs=pl.BlockSpec((tq, LANES), lambda b, h, i: (b * nq + i, h)),
        out_shape=jax.ShapeDtypeStruct((pq.shape[0], HALF), jnp.float32),
        compiler_params=_cparams(3), name="diff_attn",
    )(*args)


def _tri(n, upper):
    r = lax.broadcasted_iota(jnp.int32, (n, n), 0)
    c = lax.broadcasted_iota(jnp.int32, (n, n), 1)
    return (c >= r) if upper else (c <= r)


def _hgrn_dir(q_ref, i_ref, f_ref, lb, s_ref, o_ref, rev, bi):
    n = B_CHUNK
    incl = _tri(n, rev)
    f = lb + (1.0 - lb) * jax.nn.sigmoid(f_ref[bi])
    k = 1.0 - f
    g = jnp.log(f)
    qh = _silu(q_ref[bi]) * (B_DK ** -0.5)
    v = i_ref[bi]
    b = _mm_hi(incl.astype(jnp.float32), g)
    mid = n // 2
    ref_row = (n - 1 - mid) if rev else mid
    last_row = 0 if rev else n - 1
    ref = b[ref_row:ref_row + 1, :]
    b_last = b[last_row:last_row + 1, :]
    q_in = qh * jnp.exp(b - ref)
    k_in = k * jnp.exp(ref - b)
    q_st = qh * jnp.exp(b)
    k_st = k * jnp.exp(b_last - b)
    decay = jnp.exp(b_last)
    stack = lambda a: jnp.concatenate([a[:, h * B_DK:(h + 1) * B_DK] for h in range(B_HEADS)], axis=0)
    rows = B_HEADS * n
    r = lax.broadcasted_iota(jnp.int32, (rows, rows), 0)
    c = lax.broadcasted_iota(jnp.int32, (rows, rows), 1)
    vis = jnp.logical_and((r // n) == (c // n), (c >= r) if rev else (c <= r))
    att = jnp.where(vis, _mm_nt(stack(q_in), stack(k_in)), 0.0)
    o_in = _mm(att, stack(v))
    outs = []
    for h in range(B_HEADS):
        sl = slice(h * B_DK, (h + 1) * B_DK)
        st = s_ref[bi, h]
        outs.append(o_in[h * n:(h + 1) * n, :] + _mm_nt(q_st[:, sl], st))
        s_ref[bi, h] = st * decay[:, sl] + _mm_tn(v[:, sl], k_st[:, sl])
    o_ref[bi] = jnp.concatenate(outs, axis=-1)


def _hgrn_kernel(qf, if_, ff, qb, ib, fb, lb_ref, s0f, s0b, of, ob, sf, sb):
    @pl.when(pl.program_id(1) == 0)
    def _():
        sf[...] = s0f[...]
        sb[...] = s0b[...]
    lb = lb_ref[...]
    for bi in range(SCAN_NB):
        _hgrn_dir(qf, if_, ff, lb, sf, of, False, bi)
        _hgrn_dir(qb, ib, fb, lb, sb, ob, True, bi)


def _hgrn_scan(p, lb, s0f, s0b, n):
    batch = p.shape[0] // n
    nc = n // B_CHUNK
    p = p.reshape(batch, n, p.shape[1])
    blk = lambda col, rev: pl.BlockSpec(
        (SCAN_NB, B_CHUNK, HALF), (lambda b, c: (b, nc - 1 - c, col)) if rev else (lambda b, c: (b, c, col)))
    st = pl.BlockSpec((SCAN_NB, B_HEADS, B_DK, B_DV), lambda b, c: (b, 0, 0, 0))
    o_sd = jax.ShapeDtypeStruct((batch, n, HALF), jnp.float32)
    s_sd = jax.ShapeDtypeStruct((batch, B_HEADS, B_DK, B_DV), jnp.float32)
    o_f, o_b, s_f, s_b = pl.pallas_call(
        _hgrn_kernel,
        grid=(batch // SCAN_NB, nc),
        in_specs=[blk(3, False), blk(4, False), blk(5, False), blk(3, True), blk(4, True), blk(6, True),
                  pl.BlockSpec((1, HALF), lambda b, c: (0, 0)), st, st],
        out_specs=[blk(0, False), blk(0, True), st, st],
        out_shape=[o_sd, o_sd, s_sd, s_sd],
        compiler_params=_cparams(2), name="hgrn_scan",
    )(p, p, p, p, p, p, lb.reshape(1, HALF), s0f, s0b)
    return o_f.reshape(batch * n, HALF), o_b.reshape(batch * n, HALF), s_f, s_b


def _gdn_prep_kernel(x_ref, prev_ref, next_ref, w_ref, o_ref, *, tiles_per_seq):
    i = pl.program_id(0)
    first = (i % tiles_per_seq) == 0
    last = (i % tiles_per_seq) == tiles_per_seq - 1
    x = x_ref[...]
    prev = jnp.where(first, 0.0, prev_ref[...])
    nxt = jnp.where(last, 0.0, next_ref[...])
    xe = jnp.concatenate([prev, x, nxt], axis=0)
    pad = C_CONV // 2
    acc = jnp.zeros_like(x)
    for j in range(C_CONV):
        acc = acc + xe[SUBLANES + j - pad:SUBLANES + j - pad + PREP_TR, :] * w_ref[j:j + 1, :]
    y = _silu(acc)
    for c in range(3 * C_HEADS):
        t = y[:, c * LANES:(c + 1) * LANES]
        if c < 2 * C_HEADS:
            t = t * lax.rsqrt(jnp.sum(t * t, axis=-1, keepdims=True) + NORM_EPS)
            if c < C_HEADS:
                t = t * (C_DK ** -0.5)
        o_ref[:, c * LANES:(c + 1) * LANES] = t


def _gdn_prep(p, conv_w, n):
    width = 3 * HALF
    nt = p.shape[0] // PREP_TR
    tps = n // PREP_TR
    rb = PREP_TR // SUBLANES
    last_blk = p.shape[0] // SUBLANES - 1
    return pl.pallas_call(
        functools.partial(_gdn_prep_kernel, tiles_per_seq=tps),
        grid=(nt,),
        in_specs=[pl.BlockSpec((PREP_TR, width), lambda i: (i, 0)),
                  pl.BlockSpec((SUBLANES, width), lambda i: (jnp.maximum(i * rb - 1, 0), 0)),
                  pl.BlockSpec((SUBLANES, width), lambda i: (jnp.minimum((i + 1) * rb, last_blk), 0)),
                  pl.BlockSpec((SUBLANES, width), lambda i: (0, 0))],
        out_specs=pl.BlockSpec((PREP_TR, width), lambda i: (i, 0)),
        out_shape=jax.ShapeDtypeStruct((p.shape[0], width), jnp.float32),
        compiler_params=_cparams(), name="gdn_prep",
    )(p, p, p, jnp.pad(conv_w, ((0, SUBLANES - C_CONV), (0, 0))))


def _gdn_dir(q_ref, k_ref, v_ref, gt_ref, nea, dtv, s_ref, o_ref, d, rev, bi):
    n = C_CHUNK
    gates = gt_ref[bi]
    x = gates + dtv
    la = nea * (jnp.maximum(x, 0.0) + jnp.log1p(jnp.exp(-jnp.abs(x))))
    beta_all = jax.nn.sigmoid(gates)
    inclf = _tri(n, rev).astype(jnp.float32)
    g_cols = _mm_hi(inclf, la)
    g_rows = lax.dot_general(la, inclf, (((0,), (1,)), ((), ())), precision=HI,
                             preferred_element_type=jnp.float32)
    last_row = 0 if rev else n - 1
    heads = range(C_HEADS)
    cols = [d * C_HEADS + h for h in heads]
    stack = lambda ref: jnp.concatenate([ref[bi, :, h * LANES:(h + 1) * LANES] for h in heads], axis=0)
    q, k, v = stack(q_ref), stack(k_ref), stack(v_ref)
    gc = jnp.concatenate([g_cols[:, c:c + 1] for c in cols], axis=0)
    gr = jnp.concatenate([g_rows[c:c + 1, :] for c in cols], axis=1)
    beta = jnp.concatenate([beta_all[:, 2 * C_HEADS + c:2 * C_HEADS + c + 1] for c in cols], axis=0)
    rows = C_HEADS * n
    r = lax.broadcasted_iota(jnp.int32, (rows, rows), 0)
    c = lax.broadcasted_iota(jnp.int32, (rows, rows), 1)
    same = (r // n) == (c // n)
    incl = jnp.logical_and(same, (c >= r) if rev else (c <= r))
    strict = jnp.logical_and(same, (c > r) if rev else (c < r))
    eye = (r == c).astype(jnp.float32)
    decay = jnp.where(incl, jnp.exp(jnp.where(incl, gc - gr, 0.0)), 0.0)
    kb = k * beta
    nmat = jnp.where(strict, _mm_nt(kb, k) * decay, 0.0)
    inv = eye - nmat
    pw = _mm_x3(nmat, nmat)
    for step in range(5):
        inv = inv + _mm_x3(inv, pw)
        if step < 4:
            pw = _mm_x3(pw, pw)
    e_gc = jnp.exp(gc)
    rhs = jnp.concatenate([v * beta, kb * e_gc], axis=-1)
    sol = _mm_x3(inv, rhs)
    u, w = sol[:, :C_DV], sol[:, C_DV:]
    hs = lambda a, h: a[h * n:(h + 1) * n, :]
    states = [s_ref[bi, h] for h in heads]
    v_new = jnp.concatenate([hs(u, h) - _mm(hs(w, h), states[h]) for h in heads], axis=0)
    att = _mm_nt(q, k) * decay
    o_in = _mm(att, v_new)
    q_st = q * e_gc
    o_ref[bi] = jnp.concatenate([hs(o_in, h) + _mm(hs(q_st, h), states[h]) for h in heads], axis=-1)
    for h in heads:
        g_last = gc[h * n + last_row:h * n + last_row + 1, :]
        k_st = hs(k, h) * jnp.exp(g_last - hs(gc, h))
        s_ref[bi, h] = states[h] * jnp.exp(g_last) + _mm_tn(k_st, hs(v_new, h))


def _gdn_kernel(qf, kf, vf, gf, qb, kb, vb, gb, nea_ref, dt_ref, s0f, s0b, of, ob, sf, sb):
    @pl.when(pl.program_id(1) == 0)
    def _():
        sf[...] = s0f[...]
        sb[...] = s0b[...]
    nea, dtv = nea_ref[...], dt_ref[...]
    for bi in range(SCAN_NB):
        _gdn_dir(qf, kf, vf, gf, nea, dtv, sf, of, 0, False, bi)
        _gdn_dir(qb, kb, vb, gb, nea, dtv, sb, ob, 1, True, bi)


def _gdn_scan(qkv, p, nea, dtv, s0f, s0b, n):
    batch = p.shape[0] // n
    nc = n // C_CHUNK
    gate_col = (ODD_IN_PAD - LANES) // LANES
    p = p.reshape(batch, n, p.shape[1])
    qkv = qkv.reshape(batch, n, qkv.shape[1])

    def blk(width, col, rev):
        return pl.BlockSpec((SCAN_NB, C_CHUNK, width),
                            (lambda b, c: (b, nc - 1 - c, col)) if rev else (lambda b, c: (b, c, col)))
    st = pl.BlockSpec((SCAN_NB, C_HEADS, C_DK, C_DV), lambda b, c: (b, 0, 0, 0))
    vec = pl.BlockSpec((1, LANES), lambda b, c: (0, 0))
    o_sd = jax.ShapeDtypeStruct((batch, n, HALF), jnp.float32)
    s_sd = jax.ShapeDtypeStruct((batch, C_HEADS, C_DK, C_DV), jnp.float32)
    o_f, o_b, s_f, s_b = pl.pallas_call(
        _gdn_kernel,
        grid=(batch // SCAN_NB, nc),
        in_specs=[blk(HALF, 0, False), blk(HALF, 1, False), blk(HALF, 2, False), blk(LANES, gate_col, False),
                  blk(HALF, 0, True), blk(HALF, 1, True), blk(HALF, 2, True), blk(LANES, gate_col, True),
                  vec, vec, st, st],
        out_specs=[blk(HALF, 0, False), blk(HALF, 0, True), st, st],
        out_shape=[o_sd, o_sd, s_sd, s_sd],
        compiler_params=_cparams(2), name="gdn_scan",
    )(qkv, qkv, qkv, p, qkv, qkv, qkv, p, nea, dtv, s0f, s0b)
    return o_f.reshape(batch * n, HALF), o_b.reshape(batch * n, HALF), s_f, s_b


def _swa_kernel(sink_ref, q_ref, kp_ref, k0_ref, kn_ref, vp_ref, v0_ref, vn_ref, kc_ref, vc_ref, o_ref, *, nblk):
    i = pl.program_id(1)
    tq = SWA_TQ
    qpos = lax.broadcasted_iota(jnp.int32, (tq, 3 * tq), 0)
    kpos = lax.broadcasted_iota(jnp.int32, (tq, 3 * tq), 1) - tq
    ok = jnp.abs(kpos - qpos) <= D_WINDOW
    ok = jnp.logical_and(ok, jnp.logical_or(kpos >= 0, i > 0))
    ok = jnp.logical_and(ok, jnp.logical_or(kpos < tq, i < nblk - 1))
    group = D_HEADS // D_KV_HEADS
    for bi in range(SCAN_NB):
        q = q_ref[bi] * (D_DH ** -0.5)
        kband = jnp.concatenate([kp_ref[bi], k0_ref[bi], kn_ref[bi]], axis=0)
        vband = jnp.concatenate([vp_ref[bi], v0_ref[bi], vn_ref[bi]], axis=0)
        kc, vc = kc_ref[bi], vc_ref[bi]
        outs = []
        for h in range(D_HEADS):
            ks = slice((h // group) * D_DH, (h // group + 1) * D_DH)
            qh = q[:, h * D_DH:(h + 1) * D_DH]
            s_b = jnp.where(ok, _mm_nt(qh, kband[:, ks]), -jnp.inf)
            s_c = _mm_nt(qh, kc[:, ks])
            sink = sink_ref[h]
            mx = jnp.maximum(jnp.maximum(jnp.max(s_b, axis=-1, keepdims=True),
                                         jnp.max(s_c, axis=-1, keepdims=True)), sink)
            e_b = jnp.exp(s_b - mx)
            e_c = jnp.exp(s_c - mx)
            den = jnp.exp(sink - mx) + jnp.sum(e_b, axis=-1, keepdims=True) + jnp.sum(e_c, axis=-1, keepdims=True)
            inv = 1.0 / den
            outs.append(_mm(e_c * inv, vc[:, ks]) + _mm(e_b * inv, vband[:, ks]))
        o_ref[bi] = jnp.concatenate(outs, axis=-1)


def _window_attention(sink, p, pc, n_lat, n_ctx):
    batch = p.shape[0] // n_lat
    nblk = n_lat // SWA_TQ
    qcol = ODD_ROPE[0] // HALF
    kcol = (ODD_ROPE[0] + HALF) // LANES
    vcol = kcol + 1
    p = p.reshape(batch, n_lat, p.shape[1])
    pc = pc.reshape(batch, n_ctx, pc.shape[1])
    nb = SCAN_NB
    prev = lambda col: pl.BlockSpec((nb, SWA_TQ, LANES), lambda b, i: (b, jnp.maximum(i - 1, 0), col))
    cur = lambda col: pl.BlockSpec((nb, SWA_TQ, LANES), lambda b, i: (b, i, col))
    nxt = lambda col: pl.BlockSpec((nb, SWA_TQ, LANES), lambda b, i: (b, jnp.minimum(i + 1, nblk - 1), col))
    out = pl.pallas_call(
        functools.partial(_swa_kernel, nblk=nblk),
        grid=(batch // nb, nblk),
        in_specs=[pl.BlockSpec(memory_space=pltpu.SMEM),
                  pl.BlockSpec((nb, SWA_TQ, HALF), lambda b, i: (b, i, qcol)),
                  prev(kcol), cur(kcol), nxt(kcol), prev(vcol), cur(vcol), nxt(vcol),
                  pl.BlockSpec((nb, n_ctx, LANES), lambda b, i: (b, 0, kcol)),
                  pl.BlockSpec((nb, n_ctx, LANES), lambda b, i: (b, 0, vcol))],
        out_specs=pl.BlockSpec((nb, SWA_TQ, HALF), lambda b, i: (b, i, 0)),
        out_shape=jax.ShapeDtypeStruct((batch, n_lat, HALF), jnp.float32),
        compiler_params=_cparams(2), name="window_attn",
    )(sink, p, p, p, p, p, p, p, pc, pc)
    return out.reshape(batch * n_lat, HALF)


def _outproj_kernel(att_ref, of_ref, ob_ref, g_ref, nw_ref, w_ref, x_ref, gate_ref, o_ref, *, group, rec_first):
    o = of_ref[...] + ob_ref[...]
    sq = o * o
    if group == LANES:
        parts = [jnp.broadcast_to(jnp.mean(sq[:, c * LANES:(c + 1) * LANES], axis=-1, keepdims=True),
                                  (sq.shape[0], LANES)) for c in range(HALF // LANES)]
        ms = jnp.concatenate(parts, axis=-1)
    else:
        r = lax.broadcasted_iota(jnp.int32, (HALF, HALF), 0) // group
        c = lax.broadcasted_iota(jnp.int32, (HALF, HALF), 1) // group
        ms = _mm_hi(sq, (r == c).astype(jnp.float32)) * (1.0 / group)
    rec = o * lax.rsqrt(ms + NORM_EPS) * nw_ref[...] * _silu(g_ref[...])
    att = att_ref[...]
    top, bot = (rec, att) if rec_first else (att, rec)
    y = _mm(top, w_ref[:HALF, :]) + _mm(bot, w_ref[HALF:, :])
    o_ref[...] = x_ref[...] + gate_ref[0] * y


def _outproj(att, of, ob, p, gate_col, norm_w, w_out, x2, gate, tokens_per_batch, group, rec_first):
    n_tok = x2.shape[0]
    tpb = tokens_per_batch // PROJ_TM
    row = lambda col: pl.BlockSpec((PROJ_TM, HALF), lambda i: (i, col))
    return pl.pallas_call(
        functools.partial(_outproj_kernel, group=group, rec_first=rec_first),
        grid=(n_tok // PROJ_TM,),
        in_specs=[row(0), row(0), row(0), row(gate_col),
                  pl.BlockSpec((1, HALF), lambda i: (0, 0)),
                  pl.BlockSpec((2 * HALF, D_MODEL), lambda i: (0, 0)),
                  pl.BlockSpec((PROJ_TM, D_MODEL), lambda i: (i, 0)),
                  pl.BlockSpec((1, 1, D_MODEL), lambda i: (i // tpb, 0, 0))],
        out_specs=pl.BlockSpec((PROJ_TM, D_MODEL), lambda i: (i, 0)),
        out_shape=jax.ShapeDtypeStruct((n_tok, D_MODEL), jnp.float32),
        compiler_params=_cparams(), name="outproj",
    )(att, of, ob, p, norm_w.reshape(1, HALF), w_out, x2, gate)


def _top16_rows(s, payload=None):
    rows = s.shape[0]
    iota = lax.broadcasted_iota(jnp.int32, s.shape, 0).astype(jnp.float32)
    vals, poss, pays = [], [], []
    for _ in range(P_TOPK):
        m = jnp.max(s, axis=0, keepdims=True)
        pos = jnp.min(jnp.where(s == m, iota, float(rows)), axis=0, keepdims=True)
        hit = iota == pos
        if payload is not None:
            pays.append(jnp.max(jnp.where(hit, payload, -1.0), axis=0, keepdims=True))
        vals.append(m)
        poss.append(pos)
        s = jnp.where(hit, -jnp.inf, s)
    cat = lambda xs: jnp.concatenate(xs, axis=0)
    return cat(vals), cat(poss), (cat(pays) if payload is not None else None)


def _route_kernel(x_ref, sh_ref, sc_ref, nw_ref, wqt_ref, keys_ref, h_ref, idx_ref, gate_ref):
    x = x_ref[...]
    h = x * lax.rsqrt(jnp.mean(x * x, axis=-1, keepdims=True) + NORM_EPS) * nw_ref[...]
    h = h * (1.0 + sc_ref[0]) + sh_ref[0]
    h_ref[...] = h
    qt = _mm_x3(wqt_ref[...], h, (((1,), (1,)), ((), ())))
    idx_rows, gate_rows = [], []
    for hh in range(P_HEADS):
        tops = []
        for p in range(2):
            j = hh * 2 + p
            s = _mm_x3(keys_ref[j], qt[j * P_NKEYS:(j + 1) * P_NKEYS, :])
            v, i, _ = _top16_rows(s)
            tops.append((v, i))
        (v0, i0), (v1, i1) = tops
        nb = [P_TOPK // (a + 1) for a in range(P_TOPK)]
        n_pad = -sum(nb) % SUBLANES
        cand = jnp.concatenate([v0[a:a + 1, :] + v1[:nb[a], :] for a in range(P_TOPK)]
                               + [jnp.full((n_pad, v0.shape[1]), -jnp.inf, jnp.float32)], axis=0)
        cidx = jnp.concatenate([i0[a:a + 1, :] * float(P_NKEYS) + i1[:nb[a], :] for a in range(P_TOPK)]
                               + [jnp.zeros((n_pad, v0.shape[1]), jnp.float32)], axis=0)
        best, _, sel = _top16_rows(cand, cidx)
        e = jnp.exp(best - best[0:1, :])
        gate_rows.append(e / jnp.sum(e, axis=0, keepdims=True))
        idx_rows.append(sel)
    idx_t = jnp.concatenate(idx_rows, axis=0).astype(jnp.int32)
    gate_t = jnp.concatenate(gate_rows, axis=0)
    idx_ref[...] = idx_t.T
    for j in range(ROUTE_TN // APPLY_TB):
        gate_ref[j] = gate_t[:, j * APPLY_TB:(j + 1) * APPLY_TB]


def _peer_route(x2, shift, scale, norm_w, wq_t, keys, tokens_per_batch):
    n_tok = x2.shape[0]
    tiles_per_batch = tokens_per_batch // ROUTE_TN
    nkeys2 = 2 * P_HEADS
    return pl.pallas_call(
        _route_kernel,
        grid=(n_tok // ROUTE_TN,),
        in_specs=[
            pl.BlockSpec((ROUTE_TN, D_MODEL), lambda i: (i, 0)),
            pl.BlockSpec((1, 1, D_MODEL), lambda i: (i // tiles_per_batch, 0, 0)),
            pl.BlockSpec((1, 1, D_MODEL), lambda i: (i // tiles_per_batch, 0, 0)),
            pl.BlockSpec((1, D_MODEL), lambda i: (0, 0)),
            pl.BlockSpec((P_HEADS * P_DQ, D_MODEL), lambda i: (0, 0)),
            pl.BlockSpec((nkeys2, P_NKEYS, P_DQ // 2), lambda i: (0, 0, 0)),
        ],
        out_specs=[
            pl.BlockSpec((ROUTE_TN, D_MODEL), lambda i: (i, 0)),
            pl.BlockSpec((ROUTE_TN, P_NSEL), lambda i: (i, 0)),
            pl.BlockSpec((ROUTE_TN // APPLY_TB, P_NSEL, APPLY_TB), lambda i: (i, 0, 0)),
        ],
        out_shape=[
            jax.ShapeDtypeStruct((n_tok, D_MODEL), jnp.float32),
            jax.ShapeDtypeStruct((n_tok, P_NSEL), jnp.int32),
            jax.ShapeDtypeStruct((n_tok // APPLY_TB, P_NSEL, APPLY_TB), jnp.float32),
        ],
        compiler_params=_cparams(), name="peer_route",
    )(x2, shift, scale, norm_w.reshape(1, D_MODEL), wq_t, keys.reshape(nkeys2, P_NKEYS, P_DQ // 2))


def _apply_kernel(idx_cur, idx_nxt, h_ref, gate_ref, x_ref, g2_ref, tab_ref, o_ref, buf, sem):
    i = pl.program_id(0)
    n = pl.num_programs(0)
    slot = i % 2

    def row_copy(idx_ref, s, t, r):
        return pltpu.make_async_copy(
            tab_ref.at[idx_ref[t, r]],
            buf.at[s, t * ROW_GROUPS + r // SUBLANES, :, r % SUBLANES, :],
            sem.at[s])

    def issue(idx_ref, s):
        def body(t, carry):
            for r in range(P_NSEL):
                row_copy(idx_ref, s, t, r).start(priority=r % 2)
            return carry
        lax.fori_loop(0, APPLY_TB, body, 0)

    @pl.when(i == 0)
    def _():
        issue(idx_cur, 0)

    @pl.when(i + 1 < n)
    def _():
        issue(idx_nxt, 1 - slot)

    pltpu.make_async_copy(buf.at[slot], buf.at[slot], sem.at[slot]).wait()

    half = ROW_TILES // 2
    lane = lax.broadcasted_iota(jnp.int32, (P_NSEL, APPLY_TB), 1)
    a = jnp.zeros((P_NSEL, APPLY_TB), jnp.float32)
    for t in range(APPLY_TB):
        u = buf[slot, t * ROW_GROUPS:(t + 1) * ROW_GROUPS, :half]
        xt = h_ref[t:t + 1, :].reshape(1, half, 1, LANES)
        s = jnp.sum(u * xt, axis=1)
        at = jnp.sum(s, axis=-1, keepdims=True).reshape(P_NSEL, 1)
        a = jnp.where(lane == t, at, a)
    coef = gate_ref[0] * (0.5 * a * (1.0 + lax.erf(a * (2.0 ** -0.5))))
    for t in range(APPLY_TB):
        v = buf[slot, t * ROW_GROUPS:(t + 1) * ROW_GROUPS, half:]
        ct = coef[:, t:t + 1].reshape(ROW_GROUPS, 1, SUBLANES, 1)
        o = jnp.sum(jnp.sum(ct * v, axis=0), axis=1)
        o_ref[t:t + 1, :] = x_ref[t:t + 1, :] + g2_ref[0] * o.reshape(1, D_MODEL)


def _peer_apply(idx, h, gate_t, x2, g2, tab, tokens_per_batch):
    n_tok = h.shape[0]
    nt = n_tok // APPLY_TB
    tpb = tokens_per_batch // APPLY_TB
    return pl.pallas_call(
        _apply_kernel,
        grid=(nt,),
        in_specs=[
            pl.BlockSpec((APPLY_TB, P_NSEL), lambda i: (i, 0), memory_space=pltpu.SMEM),
            pl.BlockSpec((APPLY_TB, P_NSEL), lambda i: (jnp.minimum(i + 1, nt - 1), 0), memory_space=pltpu.SMEM),
            pl.BlockSpec((APPLY_TB, D_MODEL), lambda i: (i, 0)),
            pl.BlockSpec((1, P_NSEL, APPLY_TB), lambda i: (i, 0, 0)),
            pl.BlockSpec((APPLY_TB, D_MODEL), lambda i: (i, 0)),
            pl.BlockSpec((1, 1, D_MODEL), lambda i: (i // tpb, 0, 0)),
            pl.BlockSpec(memory_space=pl.ANY),
        ],
        out_specs=pl.BlockSpec((APPLY_TB, D_MODEL), lambda i: (i, 0)),
        out_shape=jax.ShapeDtypeStruct((n_tok, D_MODEL), jnp.float32),
        scratch_shapes=[pltpu.VMEM((2, APPLY_TB * ROW_GROUPS, ROW_TILES, SUBLANES, LANES), jnp.float32),
                        pltpu.SemaphoreType.DMA((2,))],
        compiler_params=_cparams(), name="peer_apply",
    )(idx, idx, h, gate_t, x2, g2, tab)


def _peer_block(x2, norm_w, shift, scale, g2, wq_t, keys, tab, tokens_per_batch):
    h, idx, gate_t = _peer_route(x2, shift, scale, norm_w, wq_t, keys, tokens_per_batch)
    return _peer_apply(idx, h, gate_t, x2, g2, tab, tokens_per_batch)


def _final_norm_kernel(x_ref, w_ref, o_ref):
    x = x_ref[...]
    o_ref[...] = x * lax.rsqrt(jnp.mean(x * x, axis=-1, keepdims=True) + NORM_EPS) * w_ref[...]


def _final_norm(x2, w):
    tm = 512
    return pl.pallas_call(
        _final_norm_kernel,
        grid=(x2.shape[0] // tm,),
        in_specs=[pl.BlockSpec((tm, D_MODEL), lambda i: (i, 0)), pl.BlockSpec((1, D_MODEL), lambda i: (0, 0))],
        out_specs=pl.BlockSpec((tm, D_MODEL), lambda i: (i, 0)),
        out_shape=jax.ShapeDtypeStruct(x2.shape, jnp.float32),
        compiler_params=_cparams(), name="final_norm",
    )(x2, w.reshape(1, D_MODEL))


def _even_layer(layer, x2, xc2, mods, modc, norm_w, w_in, w_out, diff_lambda, subln_w, hgrn_lb, hgrn_norm_w,
                rope_tabs, T, L, need_ctx):
    B = x2.shape[0] // T
    sh1, sc1, g1 = mods
    sh1c, sc1c, g1c = modc
    w_in_b = w_in.astype(MXU_DT)
    p = _inproj(x2, sh1, sc1, norm_w, w_in_b, T, rope=EVEN_ROPE, rope_tabs=rope_tabs)
    pc = _inproj(xc2, sh1c, sc1c, norm_w, w_in_b, L)

    lambda_init = LAMBDA_INIT_BASE - LAMBDA_INIT_SPAN * math.exp(-LAMBDA_INIT_RATE * layer)
    lam = (jnp.exp(jnp.sum(diff_lambda[0] * diff_lambda[1])) - jnp.exp(jnp.sum(diff_lambda[2] * diff_lambda[3]))
           + lambda_init).reshape(1)
    o_a = _diff_attention(lam, p, pc, p, subln_w, T, L, T, 1.0 - lambda_init)

    lb = jnp.cumsum(jax.nn.softmax(hgrn_lb, axis=0), axis=0)[layer]
    s0 = jnp.zeros((B, B_HEADS, B_DK, B_DV), jnp.float32)
    oc_f, oc_b, sf, sb = _hgrn_scan(pc, lb, s0, s0, L)
    o_f, o_b, _, _ = _hgrn_scan(p, lb, sf, sb, T)

    w_out_b = w_out.astype(MXU_DT)
    nw_t = jnp.tile(hgrn_norm_w, B_HEADS)
    x2 = _outproj(o_a, o_f, o_b, p, 7, nw_t, w_out_b, x2, g1, T, B_DV, False)
    if need_ctx:
        oc_a = _diff_attention(lam, pc, pc, None, subln_w, L, L, 0, 1.0 - lambda_init)
        xc2 = _outproj(oc_a, oc_f, oc_b, pc, 7, nw_t, w_out_b, xc2, g1c, L, B_DV, False)
    return x2, xc2


def _odd_layer(x2, xc2, mods, modc, norm_w, w_in, w_out, conv_w, a_log, dt_bias, gdn_norm_w, sink, rope_tabs, T, L):
    B = x2.shape[0] // T
    sh1, sc1, g1 = mods
    sh1c, sc1c, _ = modc
    c_qkv, c_z, c_g, c_q, c_kv = np.cumsum((0, C_HEADS * (2 * C_DK + C_DV), C_HEADS * C_DV, 4 * C_HEADS,
                                            D_HEADS * D_DH))[:5]
    end = c_kv + 2 * D_KV_HEADS * D_DH
    w_r = jnp.concatenate([w_in[:, c_qkv:c_g], w_in[:, c_q:end], w_in[:, c_g:c_q],
                           jnp.zeros((D_MODEL, ODD_IN_PAD - end), w_in.dtype)], axis=1).astype(MXU_DT)
    p = _inproj(x2, sh1, sc1, norm_w, w_r, T, rope=ODD_ROPE, rope_tabs=rope_tabs)
    pc = _inproj(xc2, sh1c, sc1c, norm_w, w_r, L)

    qkv = _gdn_prep(p, conv_w, T)
    qkvc = _gdn_prep(pc, conv_w, L)
    pad = lambda v: jnp.pad(v.reshape(1, -1), ((0, 0), (0, LANES - 2 * C_HEADS)))
    nea = pad(-jnp.exp(a_log))
    dtv = pad(dt_bias)
    s0 = jnp.zeros((B, C_HEADS, C_DK, C_DV), jnp.float32)
    _, _, sf, sb = _gdn_scan(qkvc, pc, nea, dtv, s0, s0, L)
    o_f, o_b, _, _ = _gdn_scan(qkv, p, nea, dtv, sf, sb, T)

    o_d = _window_attention(sink, p, pc, T, L)
    x2 = _outproj(o_d, o_f, o_b, p, 3, jnp.tile(gdn_norm_w, C_HEADS), w_out.astype(MXU_DT), x2, g1, T, C_DV, True)
    return x2


def kernel(x, c, ctx, c_ctx, ada_w, ada_b, norm_w, final_norm_w, even_w_in, even_w_out, diff_lambda, diff_subln_w, hgrn_lb, hgrn_norm_w, odd_w_in, odd_w_out, gdn_conv_w, gdn_a_log, gdn_dt_bias, gdn_norm_w, swa_sink, peer_w_q, peer_keys, peer_u, peer_v):
    B, T, D = x.shape
    L = ctx.shape[1]
    rope_tabs = _rope_tables(T)
    x2 = x.reshape(B * T, D)
    xc2 = ctx.reshape(B * L, D)
    cvecs = jnp.concatenate([c, jnp.broadcast_to(c_ctx[None, :], (SUBLANES, D))], axis=0)
    for layer in range(DEPTH):
        need_ctx = layer < DEPTH - 1
        ada = _ada_linear(cvecs, ada_w[layer], ada_b[layer])
        lat = [ada[:B, j * D:(j + 1) * D].reshape(B, 1, D) for j in range(ADA_CHUNKS)]
        cx = [jnp.broadcast_to(ada[B:B + 1, j * D:(j + 1) * D].reshape(1, 1, D), (B, 1, D)) for j in range(ADA_CHUNKS)]
        if layer % 2 == 0:
            e = layer // 2
            x2, xc2 = _even_layer(layer, x2, xc2, lat[:3], cx[:3], norm_w[layer, 0], even_w_in[e], even_w_out[e],
                                  diff_lambda[e], diff_subln_w[e], hgrn_lb, hgrn_norm_w[e], rope_tabs, T, L, need_ctx)
        else:
            o = layer // 2
            x2 = _odd_layer(x2, xc2, lat[:3], cx[:3], norm_w[layer, 0], odd_w_in[o], odd_w_out[o], gdn_conv_w[o],
                            gdn_a_log[o], gdn_dt_bias[o], gdn_norm_w[o], swa_sink[o], rope_tabs, T, L)
        wq_t = peer_w_q[layer].T
        tab = jnp.concatenate([peer_u[layer], peer_v[layer]], axis=-1).reshape(-1, ROW_TILES, LANES)
        x2 = _peer_block(x2, norm_w[layer, 1], lat[3], lat[4], lat[5], wq_t, peer_keys[layer], tab, T)
        if need_ctx:
            xc2 = _peer_block(xc2, norm_w[layer, 1], cx[3], cx[4], cx[5], wq_t, peer_keys[layer], tab, L)
    return _final_norm(x2, final_norm_w).reshape(B, T, D)
```
